```python
import jax, jax.numpy as jnp
from jax import lax
import numpy as np

D_MODEL = 1024
BATCH = 4
SEQ = 8192
DEPTH = 1
DEC_BATCH = 16
DEC_SEQ = 32
PAST_LEN = 2048

CHUNK = 64
EPS = 1e-6
ROPE_BASE = 10000.0
H_RET = 4
W_RET = D_MODEL // 2
DH_RET = W_RET // H_RET
H_ML = 4
W_ML = D_MODEL // 2
DH_ML = W_ML // H_ML
CONV_W = 4
PEER_HEADS = 8
N_KEYS = 128
N_EXPERTS = N_KEYS * N_KEYS
PEER_TOPK = 16
PEER_DQ = 256
PEER_DQ_HALF = PEER_DQ // 2
PEER_BLOCK = 256
D_PLE = 256
N_IN = 4 * W_RET + 3 * W_ML + 2 * H_ML + 2 * D_MODEL

kernel_name = "hybrid_retention_mlstm_peer_stream_step"


def rmsnorm(x, g):
    xf = x.astype(jnp.float32)
    y = xf * lax.rsqrt(jnp.mean(xf * xf, axis=-1, keepdims=True) + EPS)
    return (y * g.astype(jnp.float32)).astype(x.dtype)


def head_rmsnorm(y, g):
    B, T, H, d = y.shape
    y = y * lax.rsqrt(jnp.mean(y * y, axis=-1, keepdims=True) + EPS)
    return y.reshape(B, T, H * d) * g.astype(jnp.float32)


def rope(x, pos):
    half = x.shape[-1] // 2
    inv = ROPE_BASE ** (-jnp.arange(half, dtype=jnp.float32) / half)
    ang = pos.astype(jnp.float32)[:, None] * inv[None, :]
    cos = jnp.cos(ang)[None, :, None, :]
    sin = jnp.sin(ang)[None, :, None, :]
    x1, x2 = x[..., :half], x[..., half:]
    return jnp.concatenate([x1 * cos - x2 * sin, x1 * sin + x2 * cos], axis=-1)


def split_cols(z):
    sizes = (W_RET, W_RET, W_RET, W_RET, W_ML, W_ML, W_ML, H_ML, H_ML, D_MODEL, D_MODEL)
    out = []
    off = 0
    for s in sizes:
        out.append(z[..., off:off + s])
        off += s
    return out


def causal_conv(x, buf, w, b):
    T = x.shape[1]
    xp = jnp.concatenate([buf, x], axis=1)
    y = b + xp[:, 0:T] * w[0]
    for j in range(1, CONV_W):
        y = y + xp[:, j:j + T] * w[j]
    return y, xp[:, T:]


def retention_chunkwise(q, k, v, s0):
    B, T, H, dk = q.shape
    dv = v.shape[-1]
    L = min(CHUNK, T)
    nc = T // L
    to_c = lambda t: t.reshape(B, nc, L, H, t.shape[-1]).transpose(0, 3, 1, 2, 4)
    q, k, v = to_c(q), to_c(k), to_c(v)
    lg = jnp.log1p(-jnp.exp2(-5.0 - jnp.arange(H, dtype=jnp.float32)))
    idx = jnp.arange(L, dtype=jnp.float32)
    diff = idx[:, None] - idx[None, :]
    dmask = jnp.where(diff >= 0, jnp.exp(lg[:, None, None] * jnp.maximum(diff, 0.0)), 0.0)
    scores = jnp.einsum('bhcid,bhcsd->bhcis', q, k) * dmask[None, :, None]
    o = jnp.einsum('bhcis,bhcse->bhcie', scores, v)
    w_end = jnp.exp(lg[:, None] * (L - 1 - idx)[None, :])
    ds = jnp.einsum('bhcsd,bhcse->bhcde', k * w_end[None, :, None, :, None], v)
    g_chunk = jnp.exp(lg * L)[None, :, None, None]

    def step(s, ds_c):
        return g_chunk * s + ds_c, s

    s_fin, s_prev = lax.scan(step, s0, jnp.moveaxis(ds, 2, 0))
    s_prev = jnp.moveaxis(s_prev, 0, 2)
    w_read = jnp.exp(lg[:, None] * (idx + 1.0)[None, :])
    o = o + jnp.einsum('bhcid,bhcde->bhcie', q * w_read[None, :, None, :, None], s_prev)
    return o.transpose(0, 2, 3, 1, 4).reshape(B, T, H, dv), s_fin


def mlstm_chunkwise(q, k, v, ig, fg, c0, n0, m0):
    B, T, H, dk = q.shape
    L = min(CHUNK, T)
    nc = T // L
    to_c = lambda t: t.reshape(B, nc, L, H, t.shape[-1]).transpose(1, 0, 3, 2, 4)
    to_cg = lambda t: t.reshape(B, nc, L, H).transpose(1, 0, 3, 2)
    causal = jnp.tril(jnp.ones((L, L), dtype=bool))

    def step(carry, xs):
        c, n, m = carry
        qc, kc, vc, ic, lfc = xs
        F = jnp.cumsum(lfc, axis=-1)
        log_d = jnp.where(causal, ic[..., None, :] + F[..., :, None] - F[..., None, :], -jnp.inf)
        inter = m[..., None] + F
        m_t = jnp.maximum(inter, jnp.max(log_d, axis=-1))
        dw = jnp.exp(log_d - m_t[..., None])
        a = jnp.exp(inter - m_t)
        s = jnp.einsum('bhid,bhsd->bhis', qc, kc) * dw
        num = jnp.einsum('bhis,bhse->bhie', s, vc) + a[..., None] * jnp.einsum('bhid,bhde->bhie', qc, c)
        den = jnp.sum(s, axis=-1) + a * jnp.einsum('bhid,bhd->bhi', qc, n)
        hc = num / jnp.maximum(jnp.abs(den), jnp.exp(-m_t))[..., None]
        m_new = m_t[..., -1]
        w_end = jnp.exp(ic + F[..., -1:] - F - m_new[..., None])
        a_end = jnp.exp(m + F[..., -1] - m_new)
        kw = kc * w_end[..., None]
        c_new = a_end[..., None, None] * c + jnp.einsum('bhsd,bhse->bhde', kw, vc)
        n_new = a_end[..., None] * n + jnp.sum(kw, axis=-2)
        return (c_new, n_new, m_new), hc

    logf = jax.nn.log_sigmoid(fg)
    (c, n, m), h = lax.scan(step, (c0, n0, m0), (to_c(q), to_c(k), to_c(v), to_cg(ig), to_cg(logf)))
    h = h.transpose(1, 0, 3, 2, 4).reshape(B, T, H, -1)
    return h, c, n, m


def peer_ffn(h, w_pq, peer_keys, peer_u, peer_v):
    B, T, D = h.shape
    n = B * T
    nb = -(-n // PEER_BLOCK)
    flat = jnp.pad(h.reshape(n, D), ((0, nb * PEER_BLOCK - n), (0, 0)))

    def one_block(xb):
        q = (xb @ w_pq).reshape(PEER_BLOCK, PEER_HEADS, 2, PEER_DQ_HALF)
        s = jnp.einsum('tnhd,nhkd->tnhk', q, peer_keys).astype(jnp.float32)
        s1, i1 = lax.top_k(s[:, :, 0], PEER_TOPK)
        s2, i2 = lax.top_k(s[:, :, 1], PEER_TOPK)
        cand = (s1[..., :, None] + s2[..., None, :]).reshape(PEER_BLOCK, PEER_HEADS, PEER_TOPK * PEER_TOPK)
        sc, ci = lax.top_k(cand, PEER_TOPK)
        e = (jnp.take_along_axis(i1, ci // PEER_TOPK, axis=-1) * N_KEYS
             + jnp.take_along_axis(i2, ci % PEER_TOPK, axis=-1))
        g = jax.nn.softmax(sc, axis=-1)
        act = jax.nn.gelu(jnp.einsum('td,tnkd->tnk', xb, peer_u[e]).astype(jnp.float32), approximate=False)
        return jnp.einsum('tnk,tnkd->td', (g * act).astype(xb.dtype), peer_v[e])

    out = lax.map(one_block, flat.reshape(nb, PEER_BLOCK, D))
    return out.reshape(nb * PEER_BLOCK, D)[:n].reshape(B, T, D)


def hybrid_layer(x, p, pos, s_ret, c_ml, n_ml, m_ml, conv_buf,
                 g_mix, w_in, g_ret_gn, w_mq, w_mk, conv_w, conv_b, b_i, b_f, g_ml_gn, w_skip,
                 w_up_r, w_up_m, w_out, g_ffn, w_pq, peer_keys, peer_u, peer_v, g_ple, w_pg, w_ple):
    B, T, _ = x.shape
    f32 = jnp.float32
    dt = x.dtype
    h = rmsnorm(x, g_mix)
    z = h @ w_in
    q_r, k_r, v_r, gt_r, xm, v_m, o_m, i_m, f_m, gate_r, gate_m = split_cols(z)
    heads = lambda t, H: t.astype(f32).reshape(B, T, H, -1)
    q_r = rope(heads(q_r, H_RET), pos)
    k_r = rope(heads(k_r, H_RET), pos) * (DH_RET ** -0.5)
    o_ret, s_ret_new = retention_chunkwise(q_r, k_r, heads(v_r, H_RET), s_ret.astype(f32))
    y_r = jax.nn.silu(gt_r.astype(f32)) * head_rmsnorm(o_ret, g_ret_gn)
    xc, conv_new = causal_conv(xm, conv_buf.astype(dt), conv_w, conv_b)
    c = jax.nn.silu(xc.astype(f32))
    ch = c.reshape(B, T, H_ML, DH_ML)
    q_m = jnp.einsum('bthd,hde->bthe', ch, w_mq.astype(f32))
    k_m = jnp.einsum('bthd,hde->bthe', ch, w_mk.astype(f32)) * (DH_ML ** -0.5)
    ig = i_m.astype(f32) + b_i.astype(f32)
    fg = f_m.astype(f32) + b_f.astype(f32)
    h_m, c_new, n_new, m_new = mlstm_chunkwise(q_m, k_m, heads(v_m, H_ML), ig, fg,
                                               c_ml.astype(f32), n_ml.astype(f32), m_ml.astype(f32))
    y_m = jax.nn.sigmoid(o_m.astype(f32)) * (head_rmsnorm(h_m, g_ml_gn) + w_skip.astype(f32) * c)
    merged = (jax.nn.sigmoid(gate_r) * (y_r.astype(dt) @ w_up_r)
              + jax.nn.sigmoid(gate_m) * (y_m.astype(dt) @ w_up_m))
    x = x + merged @ w_out
    x = x + peer_ffn(rmsnorm(x, g_ffn), w_pq, peer_keys, peer_u, peer_v)
    x = x + (p @ w_ple) * jax.nn.sigmoid(rmsnorm(x, g_ple) @ w_pg)
    return x, s_ret_new, c_new, n_new, m_new, conv_new


def setup_inputs(seed: int = 0) -> dict:
    key = jax.random.key(seed)
    ks = jax.random.split(key, 32)
    f32 = jnp.float32
    nrm = lambda k, shape, scale: scale * jax.random.normal(k, shape, f32)
    gain = lambda k, shape: 1.0 + 0.05 * jax.random.normal(k, shape, f32)
    return {
        "x_prompt": nrm(ks[0], (BATCH, SEQ, D_MODEL), 1.0),
        "x_sample": nrm(ks[1], (DEC_BATCH, DEC_SEQ, D_MODEL), 1.0),
        "p_prompt": nrm(ks[2], (DEPTH, BATCH, SEQ, D_PLE), 1.0),
        "p_sample": nrm(ks[3], (DEPTH, DEC_BATCH, DEC_SEQ, D_PLE), 1.0),
        "state_ret": nrm(ks[4], (DEPTH, DEC_BATCH, H_RET, DH_RET, DH_RET), 0.5),
        "state_mlstm_C": nrm(ks[5], (DEPTH, DEC_BATCH, H_ML, DH_ML, DH_ML), 0.3),
        "state_mlstm_n": nrm(ks[6], (DEPTH, DEC_BATCH, H_ML, DH_ML), 0.3),
        "state_mlstm_m": nrm(ks[7], (DEPTH, DEC_BATCH, H_ML), 1.0),
        "state_conv": nrm(ks[8], (DEPTH, DEC_BATCH, CONV_W - 1, W_ML), 1.0),
        "g_mix": gain(ks[9], (DEPTH, D_MODEL)),
        "w_in": nrm(ks[10], (DEPTH, D_MODEL, N_IN), D_MODEL ** -0.5),
        "g_ret_gn": gain(ks[11], (DEPTH, W_RET)),
        "w_mq": nrm(ks[12], (DEPTH, H_ML, DH_ML, DH_ML), DH_ML ** -0.5),
        "w_mk": nrm(ks[13], (DEPTH, H_ML, DH_ML, DH_ML), DH_ML ** -0.5),
        "conv_w": nrm(ks[14], (DEPTH, CONV_W, W_ML), CONV_W ** -0.5),
        "conv_b": nrm(ks[15], (DEPTH, W_ML), 0.02),
        "b_i": nrm(ks[16], (DEPTH, H_ML), 0.1),
        "b_f": jnp.linspace(3.0, 6.0, H_ML, dtype=f32)[None, :] + nrm(ks[17], (DEPTH, H_ML), 0.1),
        "g_ml_gn": gain(ks[18], (DEPTH, W_ML)),
        "w_skip": gain(ks[19], (DEPTH, W_ML)),
        "w_up_r": nrm(ks[20], (DEPTH, W_RET, D_MODEL), W_RET ** -0.5),
        "w_up_m": nrm(ks[21], (DEPTH, W_ML, D_MODEL), W_ML ** -0.5),
        "w_out": nrm(ks[22], (DEPTH, D_MODEL, D_MODEL), D_MODEL ** -0.5),
        "g_ffn": gain(ks[23], (DEPTH, D_MODEL)),
        "w_pq": nrm(ks[24], (DEPTH, D_MODEL, PEER_HEADS * PEER_DQ), D_MODEL ** -0.5),
        "peer_keys": nrm(ks[25], (DEPTH, PEER_HEADS, 2, N_KEYS, PEER_DQ_HALF), PEER_DQ_HALF ** -0.5),
        "peer_u": nrm(ks[26], (DEPTH, N_EXPERTS, D_MODEL), D_MODEL ** -0.5),
        "peer_v": nrm(ks[27], (DEPTH, N_EXPERTS, D_MODEL), 0.1),
        "g_ple": gain(ks[28], (DEPTH, D_MODEL)),
        "w_pg": nrm(ks[29], (DEPTH, D_MODEL, D_MODEL), D_MODEL ** -0.5),
        "w_ple": nrm(ks[30], (DEPTH, D_PLE, D_MODEL), D_PLE ** -0.5),
        "g_final": gain(ks[31], (D_MODEL,)),
    }


def reference(x_prompt, x_sample, p_prompt, p_sample, state_ret, state_mlstm_C, state_mlstm_n,
              state_mlstm_m, state_conv, g_mix, w_in, g_ret_gn, w_mq, w_mk, conv_w, conv_b, b_i, b_f,
              g_ml_gn, w_skip, w_up_r, w_up_m, w_out, g_ffn, w_pq, peer_keys, peer_u, peer_v,
              g_ple, w_pg, w_ple, g_final):
    f32 = jnp.float32
    Bp = x_prompt.shape[0]
    pos_p = jnp.arange(x_prompt.shape[1], dtype=jnp.int32)
    pos_s = PAST_LEN + jnp.arange(x_sample.shape[1], dtype=jnp.int32)
    z_ret = jnp.zeros((Bp, H_RET, DH_RET, DH_RET), f32)
    z_c = jnp.zeros((Bp, H_ML, DH_ML, DH_ML), f32)
    z_n = jnp.zeros((Bp, H_ML, DH_ML), f32)
    z_m = jnp.zeros((Bp, H_ML), f32)
    z_buf = jnp.zeros((Bp, CONV_W - 1, W_ML), x_prompt.dtype)
    hp, hs = x_prompt, x_sample
    rp, cp, np_, mp, bp = [], [], [], [], []
    rs, cs, ns, ms, bs = [], [], [], [], []
    for l in range(DEPTH):
        lw = (g_mix[l], w_in[l], g_ret_gn[l], w_mq[l], w_mk[l], conv_w[l], conv_b[l], b_i[l], b_f[l],
              g_ml_gn[l], w_skip[l], w_up_r[l], w_up_m[l], w_out[l], g_ffn[l], w_pq[l], peer_keys[l],
              peer_u[l], peer_v[l], g_ple[l], w_pg[l], w_ple[l])
        hp, a1, a2, a3, a4, a5 = hybrid_layer(hp, p_prompt[l], pos_p, z_ret, z_c, z_n, z_m, z_buf, *lw)
        rp.append(a1); cp.append(a2); np_.append(a3); mp.append(a4); bp.append(a5)
        hs, b1, b2, b3, b4, b5 = hybrid_layer(hs, p_sample[l], pos_s, state_ret[l], state_mlstm_C[l],
                                              state_mlstm_n[l], state_mlstm_m[l], state_conv[l], *lw)
        rs.append(b1); cs.append(b2); ns.append(b3); ms.append(b4); bs.append(b5)
    y_prompt = rmsnorm(hp, g_final)
    y_sample = rmsnorm(hs, g_final)
    ret_p = jnp.stack(rp).astype(state_ret.dtype)
    c_p = jnp.stack(cp).astype(state_mlstm_C.dtype)
    n_p = jnp.stack(np_).astype(state_mlstm_n.dtype)
    m_p = jnp.stack(mp).astype(state_mlstm_m.dtype)
    conv_p = jnp.stack(bp).astype(state_conv.dtype)
    ret_s = jnp.stack(rs).astype(state_ret.dtype)
    c_s = jnp.stack(cs).astype(state_mlstm_C.dtype)
    n_s = jnp.stack(ns).astype(state_mlstm_n.dtype)
    m_s = jnp.stack(ms).astype(state_mlstm_m.dtype)
    conv_s = jnp.stack(bs).astype(state_conv.dtype)
    return (y_prompt, y_sample, ret_p, c_p, n_p, m_p, conv_p, ret_s, c_s, n_s, m_s, conv_s)
```

```python
import jax, jax.numpy as jnp
from jax import lax
import numpy as np
from jax.experimental import pallas as pl
from jax.experimental.pallas import tpu as pltpu

D_MODEL = 1024
BATCH = 4
SEQ = 8192
DEPTH = 1
DEC_BATCH = 16
DEC_SEQ = 32
PAST_LEN = 2048

CHUNK = 64
EPS = 1e-6
ROPE_BASE = 10000.0
H_RET = 4
W_RET = D_MODEL // 2
DH_RET = W_RET // H_RET
H_ML = 4
W_ML = D_MODEL // 2
DH_ML = W_ML // H_ML
CONV_W = 4
PEER_HEADS = 8
N_KEYS = 128
N_EXPERTS = N_KEYS * N_KEYS
PEER_TOPK = 16
PEER_DQ = 256
PEER_DQ_HALF = PEER_DQ // 2
PEER_BLOCK = 256
D_PLE = 256
N_IN = 4 * W_RET + 3 * W_ML + 2 * H_ML + 2 * D_MODEL


def rmsnorm(x, g):
    xf = x.astype(jnp.float32)
    y = xf * lax.rsqrt(jnp.mean(xf * xf, axis=-1, keepdims=True) + EPS)
    return (y * g.astype(jnp.float32)).astype(x.dtype)


def head_rmsnorm(y, g):
    B, T, H, d = y.shape
    y = y * lax.rsqrt(jnp.mean(y * y, axis=-1, keepdims=True) + EPS)
    return y.reshape(B, T, H * d) * g.astype(jnp.float32)


def rope(x, pos):
    half = x.shape[-1] // 2
    inv = ROPE_BASE ** (-jnp.arange(half, dtype=jnp.float32) / half)
    ang = pos.astype(jnp.float32)[:, None] * inv[None, :]
    cos = jnp.cos(ang)[None, :, None, :]
    sin = jnp.sin(ang)[None, :, None, :]
    x1, x2 = x[..., :half], x[..., half:]
    return jnp.concatenate([x1 * cos - x2 * sin, x1 * sin + x2 * cos], axis=-1)


def split_cols(z):
    sizes = (W_RET, W_RET, W_RET, W_RET, W_ML, W_ML, W_ML, H_ML, H_ML, D_MODEL, D_MODEL)
    out = []
    off = 0
    for s in sizes:
        out.append(z[..., off:off + s])
        off += s
    return out


def causal_conv(x, buf, w, b):
    T = x.shape[1]
    xp = jnp.concatenate([buf, x], axis=1)
    y = b + xp[:, 0:T] * w[0]
    for j in range(1, CONV_W):
        y = y + xp[:, j:j + T] * w[j]
    return y, xp[:, T:]


def retention_chunkwise(q, k, v, s0):
    B, T, H, dk = q.shape
    dv = v.shape[-1]
    L = min(CHUNK, T)
    nc = T // L
    to_c = lambda t: t.reshape(B, nc, L, H, t.shape[-1]).transpose(0, 3, 1, 2, 4)
    q, k, v = to_c(q), to_c(k), to_c(v)
    lg = jnp.log1p(-jnp.exp2(-5.0 - jnp.arange(H, dtype=jnp.float32)))
    idx = jnp.arange(L, dtype=jnp.float32)
    diff = idx[:, None] - idx[None, :]
    dmask = jnp.where(diff >= 0, jnp.exp(lg[:, None, None] * jnp.maximum(diff, 0.0)), 0.0)
    scores = jnp.einsum('bhcid,bhcsd->bhcis', q, k) * dmask[None, :, None]
    o = jnp.einsum('bhcis,bhcse->bhcie', scores, v)
    w_end = jnp.exp(lg[:, None] * (L - 1 - idx)[None, :])
    ds = jnp.einsum('bhcsd,bhcse->bhcde', k * w_end[None, :, None, :, None], v)
    g_chunk = jnp.exp(lg * L)[None, :, None, None]

    def step(s, ds_c):
        return g_chunk * s + ds_c, s

    s_fin, s_prev = lax.scan(step, s0, jnp.moveaxis(ds, 2, 0))
    s_prev = jnp.moveaxis(s_prev, 0, 2)
    w_read = jnp.exp(lg[:, None] * (idx + 1.0)[None, :])
    o = o + jnp.einsum('bhcid,bhcde->bhcie', q * w_read[None, :, None, :, None], s_prev)
    return o.transpose(0, 2, 3, 1, 4).reshape(B, T, H, dv), s_fin


def mlstm_chunkwise(q, k, v, ig, fg, c0, n0, m0):
    B, T, H, dk = q.shape
    L = min(CHUNK, T)
    nc = T // L
    to_c = lambda t: t.reshape(B, nc, L, H, t.shape[-1]).transpose(1, 0, 3, 2, 4)
    to_cg = lambda t: t.reshape(B, nc, L, H).transpose(1, 0, 3, 2)
    causal = jnp.tril(jnp.ones((L, L), dtype=bool))

    def step(carry, xs):
        c, n, m = carry
        qc, kc, vc, ic, lfc = xs
        F = jnp.cumsum(lfc, axis=-1)
        log_d = jnp.where(causal, ic[..., None, :] + F[..., :, None] - F[..., None, :], -jnp.inf)
        inter = m[..., None] + F
        m_t = jnp.maximum(inter, jnp.max(log_d, axis=-1))
        dw = jnp.exp(log_d - m_t[..., None])
        a = jnp.exp(inter - m_t)
        s = jnp.einsum('bhid,bhsd->bhis', qc, kc) * dw
        num = jnp.einsum('bhis,bhse->bhie', s, vc) + a[..., None] * jnp.einsum('bhid,bhde->bhie', qc, c)
        den = jnp.sum(s, axis=-1) + a * jnp.einsum('bhid,bhd->bhi', qc, n)
        hc = num / jnp.maximum(jnp.abs(den), jnp.exp(-m_t))[..., None]
        m_new = m_t[..., -1]
        w_end = jnp.exp(ic + F[..., -1:] - F - m_new[..., None])
        a_end = jnp.exp(m + F[..., -1] - m_new)
        kw = kc * w_end[..., None]
        c_new = a_end[..., None, None] * c + jnp.einsum('bhsd,bhse->bhde', kw, vc)
        n_new = a_end[..., None] * n + jnp.sum(kw, axis=-2)
        return (c_new, n_new, m_new), hc

    logf = jax.nn.log_sigmoid(fg)
    (c, n, m), h = lax.scan(step, (c0, n0, m0), (to_c(q), to_c(k), to_c(v), to_cg(ig), to_cg(logf)))
    h = h.transpose(1, 0, 3, 2, 4).reshape(B, T, H, -1)
    return h, c, n, m


SUBLANES = 8
LANES = 128
VREG_ELEMS = SUBLANES * LANES
PEER_SLOTS = PEER_HEADS * PEER_TOPK
PEER_TOK_BLOCK = 256
HALF_EXPERTS = N_EXPERTS // 2
PEER_TABLE_BYTES = HALF_EXPERTS * VREG_ELEMS * 4
PEER_VMEM_LIMIT = PEER_TABLE_BYTES + (12 << 20)
_BITREV8 = (0, 4, 2, 6, 1, 5, 3, 7)


def _sublane_transpose_sum(prods):
    sub = lax.broadcasted_iota(jnp.int32, (SUBLANES, LANES), 0)
    level = [prods[_BITREV8[i]] for i in range(8)]
    for blk, lo in ((4, sub < 4), (2, (sub & 2) == 0), (1, (sub & 1) == 0)):
        nxt = []
        for k in range(0, len(level), 2):
            a, b = level[k], level[k + 1]
            if blk == 4:
                nxt.append(jnp.where(lo, a, b) + pltpu.roll(jnp.where(lo, b, a), 4, 0))
            else:
                nxt.append(jnp.where(lo, a + pltpu.roll(a, SUBLANES - blk, 0), b + pltpu.roll(b, blk, 0)))
        level = nxt
    return level[0]


def _peer_u_kernel(idx_ref, x_ref, tab_ref, act_ref):
    lane = lax.broadcasted_iota(jnp.int32, (PEER_SLOTS, LANES), 1)
    for lg in range(PEER_TOK_BLOCK // LANES):
        cols = pl.ds(lg * LANES, LANES)
        act_ref[:, cols] = jnp.zeros((PEER_SLOTS, LANES), jnp.float32)

        def tok_body(tl, carry):
            t = lg * LANES + tl
            xt = x_ref[t]
            groups = []
            for grp in range(PEER_SLOTS // SUBLANES):
                prods = []
                for jj in range(SUBLANES):
                    row = pl.multiple_of(idx_ref[t, grp * SUBLANES + jj], SUBLANES)
                    prods.append(tab_ref[pl.ds(row, SUBLANES), :] * xt)
                groups.append(_sublane_transpose_sum(prods))
            q = jnp.concatenate(groups, axis=0)
            s = jnp.sum(q, axis=1, keepdims=True)
            act_ref[:, cols] = jnp.where(lane == tl, s, act_ref[:, cols])
            return carry

        lax.fori_loop(0, LANES, tok_body, 0, unroll=2)


def _peer_v_kernel(idx_ref, w_ref, tab_ref, out_ref):
    n_acc = 4

    def tok_body(t, carry):
        accs = [jnp.zeros((SUBLANES, LANES), jnp.float32) for _ in range(n_acc)]
        for j in range(PEER_SLOTS):
            row = pl.multiple_of(idx_ref[t, j], SUBLANES)
            accs[j % n_acc] = accs[j % n_acc] + w_ref[t, j] * tab_ref[pl.ds(row, SUBLANES), :]
        out_ref[t] = (accs[0] + accs[1]) + (accs[2] + accs[3])
        return carry

    lax.fori_loop(0, PEER_TOK_BLOCK, tok_body, 0, unroll=2)


def _peer_w_kernel(e_ref, g_ref, a0_ref, a1_ref, w0_ref, w1_ref):
    lo = e_ref[...] < HALF_EXPERTS
    act = jnp.where(lo, a0_ref[...], a1_ref[...])
    w = g_ref[...] * (0.5 * act * (1.0 + lax.erf(act * (2.0 ** -0.5))))
    w0_ref[...] = jnp.where(lo, w, 0.0)
    w1_ref[...] = jnp.where(lo, 0.0, w)


def _smem_block():
    return pl.BlockSpec((PEER_TOK_BLOCK, PEER_SLOTS), lambda i: (i, 0), memory_space=pltpu.SMEM)


def _resident_half(half):
    return pl.BlockSpec((HALF_EXPERTS * SUBLANES, LANES), lambda i: (half, 0),
                        pipeline_mode=pl.Buffered(1))


_PEER_PARAMS = pltpu.CompilerParams(dimension_semantics=("arbitrary",), vmem_limit_bytes=PEER_VMEM_LIMIT)


def _peer_pass_u(idx, x3, table, half):
    n = x3.shape[0]
    return pl.pallas_call(
        _peer_u_kernel,
        grid=(n // PEER_TOK_BLOCK,),
        in_specs=[_smem_block(),
                  pl.BlockSpec((PEER_TOK_BLOCK, SUBLANES, LANES), lambda i: (i, 0, 0)),
                  _resident_half(half)],
        out_specs=pl.BlockSpec((PEER_SLOTS, PEER_TOK_BLOCK), lambda i: (0, i)),
        out_shape=jax.ShapeDtypeStruct((PEER_SLOTS, n), jnp.float32),
        compiler_params=_PEER_PARAMS,
        name="peer_u",
    )(idx, x3, table)


def _peer_pass_v(idx, w, table, half):
    n = idx.shape[0]
    return pl.pallas_call(
        _peer_v_kernel,
        grid=(n // PEER_TOK_BLOCK,),
        in_specs=[_smem_block(), _smem_block(), _resident_half(half)],
        out_specs=pl.BlockSpec((PEER_TOK_BLOCK, SUBLANES, LANES), lambda i: (i, 0, 0)),
        out_shape=jax.ShapeDtypeStruct((n, SUBLANES, LANES), jnp.float32),
        compiler_params=_PEER_PARAMS,
        name="peer_v",
    )(idx, w, table)


def peer_gather_ffn(xn, e, g, peer_u, peer_v):
    n = xn.shape[0]
    x3 = xn.reshape(n, SUBLANES, LANES)
    u2 = peer_u.reshape(N_EXPERTS * SUBLANES, LANES)
    v2 = peer_v.reshape(N_EXPERTS * SUBLANES, LANES)
    rows = (jnp.minimum(e, HALF_EXPERTS - 1) * SUBLANES, jnp.maximum(e - HALF_EXPERTS, 0) * SUBLANES)
    a0 = _peer_pass_u(rows[0], x3, u2, 0)
    a1 = _peer_pass_u(rows[1], x3, u2, 1)
    blk = pl.BlockSpec((PEER_SLOTS, 2 * PEER_TOK_BLOCK), lambda i: (0, i))
    w0, w1 = pl.pallas_call(
        _peer_w_kernel,
        grid=(n // (2 * PEER_TOK_BLOCK),),
        in_specs=[blk] * 4,
        out_specs=[blk] * 2,
        out_shape=[jax.ShapeDtypeStruct((PEER_SLOTS, n), jnp.float32)] * 2,
        name="peer_w",
    )(e.T, g.T, a0, a1)
    o0 = _peer_pass_v(rows[0], w0.T, v2, 0)
    o1 = _peer_pass_v(rows[1], w1.T, v2, 1)
    return (o0 + o1).reshape(n, D_MODEL)


def peer_route(xn, w_pq, peer_keys):
    n = xn.shape[0]
    q = (xn @ w_pq).reshape(n, PEER_HEADS, 2, PEER_DQ_HALF)
    s = jnp.einsum('tnhd,nhkd->tnhk', q, peer_keys).astype(jnp.float32)
    s1, i1 = lax.top_k(s[:, :, 0], PEER_TOPK)
    s2, i2 = lax.top_k(s[:, :, 1], PEER_TOPK)
    cand = (s1[..., :, None] + s2[..., None, :]).reshape(n, PEER_HEADS, PEER_TOPK * PEER_TOPK)
    sc, ci = lax.top_k(cand, PEER_TOPK)
    e = (jnp.take_along_axis(i1, ci // PEER_TOPK, axis=-1) * N_KEYS
         + jnp.take_along_axis(i2, ci % PEER_TOPK, axis=-1))
    g = jax.nn.softmax(sc, axis=-1)
    return e.reshape(n, PEER_SLOTS), g.reshape(n, PEER_SLOTS)


def peer_ffn(h, w_pq, peer_keys, peer_u, peer_v):
    e, g = peer_route(h, w_pq, peer_keys)
    return peer_gather_ffn(h, e, g, peer_u, peer_v)


def token_mix(x, pos, s_ret, c_ml, n_ml, m_ml, conv_buf,
              g_mix, w_in, g_ret_gn, w_mq, w_mk, conv_w, conv_b, b_i, b_f, g_ml_gn, w_skip,
              w_up_r, w_up_m, w_out):
    B, T, _ = x.shape
    f32 = jnp.float32
    dt = x.dtype
    h = rmsnorm(x, g_mix)
    z = h @ w_in
    q_r, k_r, v_r, gt_r, xm, v_m, o_m, i_m, f_m, gate_r, gate_m = split_cols(z)
    heads = lambda t, H: t.astype(f32).reshape(B, T, H, -1)
    q_r = rope(heads(q_r, H_RET), pos)
    k_r = rope(heads(k_r, H_RET), pos) * (DH_RET ** -0.5)
    o_ret, s_ret_new = retention_chunkwise(q_r, k_r, heads(v_r, H_RET), s_ret.astype(f32))
    y_r = jax.nn.silu(gt_r.astype(f32)) * head_rmsnorm(o_ret, g_ret_gn)
    xc, conv_new = causal_conv(xm, conv_buf.astype(dt), conv_w, conv_b)
    c = jax.nn.silu(xc.astype(f32))
    ch = c.reshape(B, T, H_ML, DH_ML)
    q_m = jnp.einsum('bthd,hde->bthe', ch, w_mq.astype(f32))
    k_m = jnp.einsum('bthd,hde->bthe', ch, w_mk.astype(f32)) * (DH_ML ** -0.5)
    ig = i_m.astype(f32) + b_i.astype(f32)
    fg = f_m.astype(f32) + b_f.astype(f32)
    h_m, c_new, n_new, m_new = mlstm_chunkwise(q_m, k_m, heads(v_m, H_ML), ig, fg,
                                               c_ml.astype(f32), n_ml.astype(f32), m_ml.astype(f32))
    y_m = jax.nn.sigmoid(o_m.astype(f32)) * (head_rmsnorm(h_m, g_ml_gn) + w_skip.astype(f32) * c)
    merged = (jax.nn.sigmoid(gate_r) * (y_r.astype(dt) @ w_up_r)
              + jax.nn.sigmoid(gate_m) * (y_m.astype(dt) @ w_up_m))
    x = x + merged @ w_out
    return x, s_ret_new, c_new, n_new, m_new, conv_new


def channel_mix(x, p, g_ffn, w_pq, peer_keys, peer_u, peer_v, g_ple, w_pg, w_ple):
    x = x + peer_ffn(rmsnorm(x, g_ffn), w_pq, peer_keys, peer_u, peer_v)
    return x + (p @ w_ple) * jax.nn.sigmoid(rmsnorm(x, g_ple) @ w_pg)


def _final_norm_kernel(x_ref, g_ref, o_ref):
    x = x_ref[...]
    o_ref[...] = x * lax.rsqrt(jnp.mean(x * x, axis=-1, keepdims=True) + EPS) * g_ref[...]


def final_norm(x, g):
    n, D = x.shape
    tm = 512
    out = pl.pallas_call(
        _final_norm_kernel,
        grid=(n // tm,),
        in_specs=[pl.BlockSpec((tm, D), lambda i: (i, 0)), pl.BlockSpec((1, D), lambda i: (0, 0))],
        out_specs=pl.BlockSpec((tm, D), lambda i: (i, 0)),
        out_shape=jax.ShapeDtypeStruct((n, D), x.dtype),
    )(x, g.reshape(1, D))
    return out


def kernel(x_prompt, x_sample, p_prompt, p_sample, state_ret, state_mlstm_C, state_mlstm_n,
           state_mlstm_m, state_conv, g_mix, w_in, g_ret_gn, w_mq, w_mk, conv_w, conv_b, b_i, b_f,
           g_ml_gn, w_skip, w_up_r, w_up_m, w_out, g_ffn, w_pq, peer_keys, peer_u, peer_v,
           g_ple, w_pg, w_ple, g_final):
    f32 = jnp.float32
    Bp = x_prompt.shape[0]
    pos_p = jnp.arange(x_prompt.shape[1], dtype=jnp.int32)
    pos_s = PAST_LEN + jnp.arange(x_sample.shape[1], dtype=jnp.int32)
    z_ret = jnp.zeros((Bp, H_RET, DH_RET, DH_RET), f32)
    z_c = jnp.zeros((Bp, H_ML, DH_ML, DH_ML), f32)
    z_n = jnp.zeros((Bp, H_ML, DH_ML), f32)
    z_m = jnp.zeros((Bp, H_ML), f32)
    z_buf = jnp.zeros((Bp, CONV_W - 1, W_ML), x_prompt.dtype)
    lw = (g_mix[0], w_in[0], g_ret_gn[0], w_mq[0], w_mk[0], conv_w[0], conv_b[0], b_i[0], b_f[0],
          g_ml_gn[0], w_skip[0], w_up_r[0], w_up_m[0], w_out[0])
    hp, a1, a2, a3, a4, a5 = token_mix(x_prompt, pos_p, z_ret, z_c, z_n, z_m, z_buf, *lw)
    hs, b1, b2, b3, b4, b5 = token_mix(x_sample, pos_s, state_ret[0], state_mlstm_C[0],
                                       state_mlstm_n[0], state_mlstm_m[0], state_conv[0], *lw)
    n_p = x_prompt.shape[0] * x_prompt.shape[1]
    x_all = jnp.concatenate([hp.reshape(n_p, D_MODEL), hs.reshape(-1, D_MODEL)], axis=0)
    p_all = jnp.concatenate([p_prompt[0].reshape(n_p, D_PLE), p_sample[0].reshape(-1, D_PLE)], axis=0)
    x_all = channel_mix(x_all, p_all, g_ffn[0], w_pq[0], peer_keys[0], peer_u[0], peer_v[0],
                        g_ple[0], w_pg[0], w_ple[0])
    y_all = final_norm(x_all, g_final)
    y_prompt = y_all[:n_p].reshape(x_prompt.shape)
    y_sample = y_all[n_p:].reshape(x_sample.shape)
    st = lambda a: a[None]
    return (y_prompt, y_sample, st(a1), st(a2), st(a3), st(a4), st(a5),
            st(b1), st(b2), st(b3), st(b4), st(b5))
```

```python
import jax, jax.numpy as jnp
from jax import lax
import numpy as np
from jax.experimental import pallas as pl
from jax.experimental.pallas import tpu as pltpu

D_MODEL = 1024
BATCH = 4
SEQ = 8192
DEPTH = 1
DEC_BATCH = 16
DEC_SEQ = 32
PAST_LEN = 2048

CHUNK = 64
EPS = 1e-6
ROPE_BASE = 10000.0
H_RET = 4
W_RET = D_MODEL // 2
DH_RET = W_RET // H_RET
H_ML = 4
W_ML = D_MODEL // 2
DH_ML = W_ML // H_ML
CONV_W = 4
PEER_HEADS = 8
N_KEYS = 128
N_EXPERTS = N_KEYS * N_KEYS
PEER_TOPK = 16
PEER_DQ = 256
PEER_DQ_HALF = PEER_DQ // 2
PEER_BLOCK = 256
D_PLE = 256
N_IN = 4 * W_RET + 3 * W_ML + 2 * H_ML + 2 * D_MODEL


def rmsnorm(x, g):
    xf = x.astype(jnp.float32)
    y = xf * lax.rsqrt(jnp.mean(xf * xf, axis=-1, keepdims=True) + EPS)
    return (y * g.astype(jnp.float32)).astype(x.dtype)


def head_rmsnorm(y, g):
    B, T, H, d = y.shape
    y = y * lax.rsqrt(jnp.mean(y * y, axis=-1, keepdims=True) + EPS)
    return y.reshape(B, T, H * d) * g.astype(jnp.float32)


def rope(x, pos):
    half = x.shape[-1] // 2
    inv = ROPE_BASE ** (-jnp.arange(half, dtype=jnp.float32) / half)
    ang = pos.astype(jnp.float32)[:, None] * inv[None, :]
    cos = jnp.cos(ang)[None, :, None, :]
    sin = jnp.sin(ang)[None, :, None, :]
    x1, x2 = x[..., :half], x[..., half:]
    return jnp.concatenate([x1 * cos - x2 * sin, x1 * sin + x2 * cos], axis=-1)


def split_cols(z):
    sizes = (W_RET, W_RET, W_RET, W_RET, W_ML, W_ML, W_ML, H_ML, H_ML, D_MODEL, D_MODEL)
    out = []
    off = 0
    for s in sizes:
        out.append(z[..., off:off + s])
        off += s
    return out


def causal_conv(x, buf, w, b):
    T = x.shape[1]
    xp = jnp.concatenate([buf, x], axis=1)
    y = b + xp[:, 0:T] * w[0]
    for j in range(1, CONV_W):
        y = y + xp[:, j:j + T] * w[j]
    return y, xp[:, T:]


def retention_chunkwise(q, k, v, s0):
    B, T, H, dk = q.shape
    dv = v.shape[-1]
    L = min(CHUNK, T)
    nc = T // L
    to_c = lambda t: t.reshape(B, nc, L, H, t.shape[-1]).transpose(0, 3, 1, 2, 4)
    q, k, v = to_c(q), to_c(k), to_c(v)
    lg = jnp.log1p(-jnp.exp2(-5.0 - jnp.arange(H, dtype=jnp.float32)))
    idx = jnp.arange(L, dtype=jnp.float32)
    diff = idx[:, None] - idx[None, :]
    dmask = jnp.where(diff >= 0, jnp.exp(lg[:, None, None] * jnp.maximum(diff, 0.0)), 0.0)
    scores = jnp.einsum('bhcid,bhcsd->bhcis', q, k) * dmask[None, :, None]
    o = jnp.einsum('bhcis,bhcse->bhcie', scores, v)
    w_end = jnp.exp(lg[:, None] * (L - 1 - idx)[None, :])
    ds = jnp.einsum('bhcsd,bhcse->bhcde', k * w_end[None, :, None, :, None], v)
    g_chunk = jnp.exp(lg * L)[None, :, None, None]

    def step(s, ds_c):
        return g_chunk * s + ds_c, s

    s_fin, s_prev = lax.scan(step, s0, jnp.moveaxis(ds, 2, 0))
    s_prev = jnp.moveaxis(s_prev, 0, 2)
    w_read = jnp.exp(lg[:, None] * (idx + 1.0)[None, :])
    o = o + jnp.einsum('bhcid,bhcde->bhcie', q * w_read[None, :, None, :, None], s_prev)
    return o.transpose(0, 2, 3, 1, 4).reshape(B, T, H, dv), s_fin


def mlstm_chunkwise(q, k, v, ig, fg, c0, n0, m0):
    B, T, H, dk = q.shape
    L = min(CHUNK, T)
    nc = T // L
    to_c = lambda t: t.reshape(B, nc, L, H, t.shape[-1]).transpose(1, 0, 3, 2, 4)
    to_cg = lambda t: t.reshape(B, nc, L, H).transpose(1, 0, 3, 2)
    causal = jnp.tril(jnp.ones((L, L), dtype=bool))

    def step(carry, xs):
        c, n, m = carry
        qc, kc, vc, ic, lfc = xs
        F = jnp.cumsum(lfc, axis=-1)
        log_d = jnp.where(causal, ic[..., None, :] + F[..., :, None] - F[..., None, :], -jnp.inf)
        inter = m[..., None] + F
        m_t = jnp.maximum(inter, jnp.max(log_d, axis=-1))
        dw = jnp.exp(log_d - m_t[..., None])
        a = jnp.exp(inter - m_t)
        s = jnp.einsum('bhid,bhsd->bhis', qc, kc) * dw
        num = jnp.einsum('bhis,bhse->bhie', s, vc) + a[..., None] * jnp.einsum('bhid,bhde->bhie', qc, c)
        den = jnp.sum(s, axis=-1) + a * jnp.einsum('bhid,bhd->bhi', qc, n)
        hc = num / jnp.maximum(jnp.abs(den), jnp.exp(-m_t))[..., None]
        m_new = m_t[..., -1]
        w_end = jnp.exp(ic + F[..., -1:] - F - m_new[..., None])
        a_end = jnp.exp(m + F[..., -1] - m_new)
        kw = kc * w_end[..., None]
        c_new = a_end[..., None, None] * c + jnp.einsum('bhsd,bhse->bhde', kw, vc)
        n_new = a_end[..., None] * n + jnp.sum(kw, axis=-2)
        return (c_new, n_new, m_new), hc

    logf = jax.nn.log_sigmoid(fg)
    (c, n, m), h = lax.scan(step, (c0, n0, m0), (to_c(q), to_c(k), to_c(v), to_cg(ig), to_cg(logf)))
    h = h.transpose(1, 0, 3, 2, 4).reshape(B, T, H, -1)
    return h, c, n, m


SUBLANES = 8
LANES = 128
VREG_ELEMS = SUBLANES * LANES
PEER_SLOTS = PEER_HEADS * PEER_TOPK
PEER_TOK_BLOCK = 256
HALF_EXPERTS = N_EXPERTS // 2
PEER_TABLE_BYTES = HALF_EXPERTS * VREG_ELEMS * 4
PEER_VMEM_LIMIT = PEER_TABLE_BYTES + (12 << 20)
_BITREV8 = (0, 4, 2, 6, 1, 5, 3, 7)


def _sublane_transpose_sum(prods):
    sub = lax.broadcasted_iota(jnp.int32, (SUBLANES, LANES), 0)
    level = [prods[_BITREV8[i]] for i in range(8)]
    for blk, lo in ((4, sub < 4), (2, (sub & 2) == 0), (1, (sub & 1) == 0)):
        nxt = []
        for k in range(0, len(level), 2):
            a, b = level[k], level[k + 1]
            if blk == 4:
                nxt.append(jnp.where(lo, a, b) + pltpu.roll(jnp.where(lo, b, a), 4, 0))
            else:
                nxt.append(jnp.where(lo, a + pltpu.roll(a, SUBLANES - blk, 0), b + pltpu.roll(b, blk, 0)))
        level = nxt
    return level[0]


def _peer_u_kernel(idx_ref, x_ref, tab_ref, act_ref):
    lane = lax.broadcasted_iota(jnp.int32, (PEER_SLOTS, LANES), 1)
    for lg in range(PEER_TOK_BLOCK // LANES):
        cols = pl.ds(lg * LANES, LANES)
        act_ref[:, cols] = jnp.zeros((PEER_SLOTS, LANES), jnp.float32)

        def tok_body(tl, carry):
            t = lg * LANES + tl
            xt = x_ref[t]
            groups = []
            for grp in range(PEER_SLOTS // SUBLANES):
                prods = []
                for jj in range(SUBLANES):
                    row = pl.multiple_of(idx_ref[t, grp * SUBLANES + jj], SUBLANES)
                    prods.append(tab_ref[pl.ds(row, SUBLANES), :] * xt)
                groups.append(_sublane_transpose_sum(prods))
            q = jnp.concatenate(groups, axis=0)
            s = jnp.sum(q, axis=1, keepdims=True)
            act_ref[:, cols] = jnp.where(lane == tl, s, act_ref[:, cols])
            return carry

        lax.fori_loop(0, LANES, tok_body, 0, unroll=2)


def _peer_v_kernel(idx_ref, w_ref, tab_ref, out_ref):
    n_acc = 4

    def tok_body(t, carry):
        accs = [jnp.zeros((SUBLANES, LANES), jnp.float32) for _ in range(n_acc)]
        for j in range(PEER_SLOTS):
            row = pl.multiple_of(idx_ref[t, j], SUBLANES)
            accs[j % n_acc] = accs[j % n_acc] + w_ref[t, j] * tab_ref[pl.ds(row, SUBLANES), :]
        out_ref[t] = (accs[0] + accs[1]) + (accs[2] + accs[3])
        return carry

    lax.fori_loop(0, PEER_TOK_BLOCK, tok_body, 0, unroll=2)


def _peer_w_kernel(e_ref, g_ref, a0_ref, a1_ref, w0_ref, w1_ref):
    lo = e_ref[...] < HALF_EXPERTS
    act = jnp.where(lo, a0_ref[...], a1_ref[...])
    w = g_ref[...] * (0.5 * act * (1.0 + lax.erf(act * (2.0 ** -0.5))))
    w0_ref[...] = jnp.where(lo, w, 0.0)
    w1_ref[...] = jnp.where(lo, 0.0, w)


def _smem_block():
    return pl.BlockSpec((PEER_TOK_BLOCK, PEER_SLOTS), lambda i: (i, 0), memory_space=pltpu.SMEM)


def _resident_half(half):
    return pl.BlockSpec((HALF_EXPERTS * SUBLANES, LANES), lambda i: (half, 0),
                        pipeline_mode=pl.Buffered(1))


_PEER_PARAMS = pltpu.CompilerParams(dimension_semantics=("arbitrary",), vmem_limit_bytes=PEER_VMEM_LIMIT)


def _peer_pass_u(idx, x3, table, half):
    n = x3.shape[0]
    return pl.pallas_call(
        _peer_u_kernel,
        grid=(n // PEER_TOK_BLOCK,),
        in_specs=[_smem_block(),
                  pl.BlockSpec((PEER_TOK_BLOCK, SUBLANES, LANES), lambda i: (i, 0, 0)),
                  _resident_half(half)],
        out_specs=pl.BlockSpec((PEER_SLOTS, PEER_TOK_BLOCK), lambda i: (0, i)),
        out_shape=jax.ShapeDtypeStruct((PEER_SLOTS, n), jnp.float32),
        compiler_params=_PEER_PARAMS,
        name="peer_u",
    )(idx, x3, table)


def _peer_pass_v(idx, w, table, half):
    n = idx.shape[0]
    return pl.pallas_call(
        _peer_v_kernel,
        grid=(n // PEER_TOK_BLOCK,),
        in_specs=[_smem_block(), _smem_block(), _resident_half(half)],
        out_specs=pl.BlockSpec((PEER_TOK_BLOCK, SUBLANES, LANES), lambda i: (i, 0, 0)),
        out_shape=jax.ShapeDtypeStruct((n, SUBLANES, LANES), jnp.float32),
        compiler_params=_PEER_PARAMS,
        name="peer_v",
    )(idx, w, table)


def peer_gather_ffn(xn, e_t, g_t, peer_u, peer_v):
    n = xn.shape[0]
    x3 = xn.reshape(n, SUBLANES, LANES)
    u2 = peer_u.reshape(N_EXPERTS * SUBLANES, LANES)
    v2 = peer_v.reshape(N_EXPERTS * SUBLANES, LANES)
    e = e_t.T
    rows = (jnp.minimum(e, HALF_EXPERTS - 1) * SUBLANES, jnp.maximum(e - HALF_EXPERTS, 0) * SUBLANES)
    a0 = _peer_pass_u(rows[0], x3, u2, 0)
    a1 = _peer_pass_u(rows[1], x3, u2, 1)
    blk = pl.BlockSpec((PEER_SLOTS, 2 * PEER_TOK_BLOCK), lambda i: (0, i))
    w0, w1 = pl.pallas_call(
        _peer_w_kernel,
        grid=(n // (2 * PEER_TOK_BLOCK),),
        in_specs=[blk] * 4,
        out_specs=[blk] * 2,
        out_shape=[jax.ShapeDtypeStruct((PEER_SLOTS, n), jnp.float32)] * 2,
        name="peer_w",
    )(e_t, g_t, a0, a1)
    o0 = _peer_pass_v(rows[0], w0.T, v2, 0)
    o1 = _peer_pass_v(rows[1], w1.T, v2, 1)
    return (o0 + o1).reshape(n, D_MODEL)


ROUTE_TOK_BLOCK = 256
NEG_INF = float("-inf")
_CAND_GROUPS = ((0, 0), (0, 8), (1, 0)) + tuple((a, 0) for a in range(2, 8))


def _take_top16_rows(s, code):
    big = jnp.float32(1e9)
    vals, codes = [], []
    for _ in range(PEER_TOPK):
        m = jnp.max(s, axis=0, keepdims=True)
        c = jnp.min(jnp.where(s == m, code, big), axis=0, keepdims=True)
        s = jnp.where(code == c, NEG_INF, s)
        vals.append(m)
        codes.append(c)
    return jnp.concatenate(vals, axis=0), jnp.concatenate(codes, axis=0)


def _lookup_rows(table, sel):
    out = jnp.zeros_like(sel)
    for a in range(PEER_TOPK):
        out = jnp.where(sel == jnp.float32(a), table[a:a + 1, :], out)
    return out


def _route_one_head(s1, s2):
    key_id = lax.broadcasted_iota(jnp.int32, (N_KEYS, LANES), 0).astype(jnp.float32)
    v1, i1 = _take_top16_rows(s1, key_id)
    v2, i2 = _take_top16_rows(s2, key_id)
    sub = lax.broadcasted_iota(jnp.int32, (SUBLANES, LANES), 0)
    subf = sub.astype(jnp.float32)
    cand, code = [], []
    for a, b0 in _CAND_GROUPS:
        c = v1[a:a + 1, :] + v2[b0:b0 + SUBLANES, :]
        n_valid = PEER_TOPK // (a + 1) - b0
        cand.append(c if n_valid >= SUBLANES else jnp.where(sub < n_valid, c, NEG_INF))
        code.append(subf + jnp.float32(a * PEER_TOPK + b0))
    cand.append(v1[SUBLANES:, :] + v2[0:1, :])
    code.append((subf + jnp.float32(SUBLANES)) * jnp.float32(PEER_TOPK))
    sc, cc = _take_top16_rows(jnp.concatenate(cand, axis=0), jnp.concatenate(code, axis=0))
    a_sel = jnp.floor(cc * jnp.float32(1.0 / PEER_TOPK))
    b_sel = cc - a_sel * jnp.float32(PEER_TOPK)
    e = _lookup_rows(i1, a_sel) * jnp.float32(N_KEYS) + _lookup_rows(i2, b_sel)
    ex = jnp.exp(sc - sc[0:1, :])
    gate = ex / jnp.sum(ex, axis=0, keepdims=True)
    return e.astype(jnp.int32), gate


def _peer_route_kernel(x_ref, g_ref, wpq_ref, keys_ref, xn_ref, e_ref, gate_ref, q_scr):
    x = x_ref[...]
    xn = x * lax.rsqrt(jnp.mean(x * x, axis=-1, keepdims=True) + EPS) * g_ref[...]
    xn_ref[...] = xn
    q = jnp.dot(xn.astype(jnp.bfloat16), wpq_ref[...], preferred_element_type=jnp.float32)
    for c in range(2 * PEER_HEADS):
        q_scr[c] = q[:, c * PEER_DQ_HALF:(c + 1) * PEER_DQ_HALF].astype(jnp.bfloat16)

    def head_body(n, carry):
        rows = pl.ds(pl.multiple_of(n * PEER_TOPK, PEER_TOPK), PEER_TOPK)
        for lg in range(ROUTE_TOK_BLOCK // LANES):
            toks = slice(lg * LANES, (lg + 1) * LANES)
            s = [lax.dot_general(keys_ref[2 * n + h], q_scr[2 * n + h, toks, :],
                                 (((1,), (1,)), ((), ())), preferred_element_type=jnp.float32)
                 for h in range(2)]
            e, gate = _route_one_head(s[0], s[1])
            e_ref[rows, toks] = e
            gate_ref[rows, toks] = gate
        return carry

    lax.fori_loop(0, PEER_HEADS, head_body, 0)


def peer_route(x, g_ffn, w_pq, peer_keys):
    n = x.shape[0]
    tok = lambda i: (i, 0)
    fixed2 = lambda i: (0, 0)
    slots = pl.BlockSpec((PEER_SLOTS, ROUTE_TOK_BLOCK), lambda i: (0, i))
    return pl.pallas_call(
        _peer_route_kernel,
        grid=(n // ROUTE_TOK_BLOCK,),
        in_specs=[pl.BlockSpec((ROUTE_TOK_BLOCK, D_MODEL), tok),
                  pl.BlockSpec((1, D_MODEL), fixed2),
                  pl.BlockSpec((D_MODEL, PEER_HEADS * PEER_DQ), fixed2),
                  pl.BlockSpec((2 * PEER_HEADS, N_KEYS, PEER_DQ_HALF), lambda i: (0, 0, 0))],
        out_specs=[pl.BlockSpec((ROUTE_TOK_BLOCK, D_MODEL), tok), slots, slots],
        out_shape=[jax.ShapeDtypeStruct((n, D_MODEL), jnp.float32),
                   jax.ShapeDtypeStruct((PEER_SLOTS, n), jnp.int32),
                   jax.ShapeDtypeStruct((PEER_SLOTS, n), jnp.float32)],
        scratch_shapes=[pltpu.VMEM((2 * PEER_HEADS, ROUTE_TOK_BLOCK, PEER_DQ_HALF), jnp.bfloat16)],
        compiler_params=pltpu.CompilerParams(dimension_semantics=("arbitrary",),
                                             vmem_limit_bytes=40 << 20),
        name="peer_route",
    )(x, g_ffn.reshape(1, D_MODEL), w_pq.astype(jnp.bfloat16),
      peer_keys.reshape(2 * PEER_HEADS, N_KEYS, PEER_DQ_HALF).astype(jnp.bfloat16))


def peer_ffn(x, g_ffn, w_pq, peer_keys, peer_u, peer_v):
    xn, e_t, g_t = peer_route(x, g_ffn, w_pq, peer_keys)
    return peer_gather_ffn(xn, e_t, g_t, peer_u, peer_v)


def token_mix(x, pos, s_ret, c_ml, n_ml, m_ml, conv_buf,
              g_mix, w_in, g_ret_gn, w_mq, w_mk, conv_w, conv_b, b_i, b_f, g_ml_gn, w_skip,
              w_up_r, w_up_m, w_out):
    B, T, _ = x.shape
    f32 = jnp.float32
    dt = x.dtype
    h = rmsnorm(x, g_mix)
    z = h @ w_in
    q_r, k_r, v_r, gt_r, xm, v_m, o_m, i_m, f_m, gate_r, gate_m = split_cols(z)
    heads = lambda t, H: t.astype(f32).reshape(B, T, H, -1)
    q_r = rope(heads(q_r, H_RET), pos)
    k_r = rope(heads(k_r, H_RET), pos) * (DH_RET ** -0.5)
    o_ret, s_ret_new = retention_chunkwise(q_r, k_r, heads(v_r, H_RET), s_ret.astype(f32))
    y_r = jax.nn.silu(gt_r.astype(f32)) * head_rmsnorm(o_ret, g_ret_gn)
    xc, conv_new = causal_conv(xm, conv_buf.astype(dt), conv_w, conv_b)
    c = jax.nn.silu(xc.astype(f32))
    ch = c.reshape(B, T, H_ML, DH_ML)
    q_m = jnp.einsum('bthd,hde->bthe', ch, w_mq.astype(f32))
    k_m = jnp.einsum('bthd,hde->bthe', ch, w_mk.astype(f32)) * (DH_ML ** -0.5)
    ig = i_m.astype(f32) + b_i.astype(f32)
    fg = f_m.astype(f32) + b_f.astype(f32)
    h_m, c_new, n_new, m_new = mlstm_chunkwise(q_m, k_m, heads(v_m, H_ML), ig, fg,
                                               c_ml.astype(f32), n_ml.astype(f32), m_ml.astype(f32))
    y_m = jax.nn.sigmoid(o_m.astype(f32)) * (head_rmsnorm(h_m, g_ml_gn) + w_skip.astype(f32) * c)
    merged = (jax.nn.sigmoid(gate_r) * (y_r.astype(dt) @ w_up_r)
              + jax.nn.sigmoid(gate_m) * (y_m.astype(dt) @ w_up_m))
    x = x + merged @ w_out
    return x, s_ret_new, c_new, n_new, m_new, conv_new


def channel_mix(x, p, g_ffn, w_pq, peer_keys, peer_u, peer_v, g_ple, w_pg, w_ple):
    x = x + peer_ffn(x, g_ffn, w_pq, peer_keys, peer_u, peer_v)
    return x + (p @ w_ple) * jax.nn.sigmoid(rmsnorm(x, g_ple) @ w_pg)


def _final_norm_kernel(x_ref, g_ref, o_ref):
    x = x_ref[...]
    o_ref[...] = x * lax.rsqrt(jnp.mean(x * x, axis=-1, keepdims=True) + EPS) * g_ref[...]


def final_norm(x, g):
    n, D = x.shape
    tm = 512
    out = pl.pallas_call(
        _final_norm_kernel,
        grid=(n // tm,),
        in_specs=[pl.BlockSpec((tm, D), lambda i: (i, 0)), pl.BlockSpec((1, D), lambda i: (0, 0))],
        out_specs=pl.BlockSpec((tm, D), lambda i: (i, 0)),
        out_shape=jax.ShapeDtypeStruct((n, D), x.dtype),
    )(x, g.reshape(1, D))
    return out


def kernel(x_prompt, x_sample, p_prompt, p_sample, state_ret, state_mlstm_C, state_mlstm_n,
           state_mlstm_m, state_conv, g_mix, w_in, g_ret_gn, w_mq, w_mk, conv_w, conv_b, b_i, b_f,
           g_ml_gn, w_skip, w_up_r, w_up_m, w_out, g_ffn, w_pq, peer_keys, peer_u, peer_v,
           g_ple, w_pg, w_ple, g_final):
    f32 = jnp.float32
    Bp = x_prompt.shape[0]
    pos_p = jnp.arange(x_prompt.shape[1], dtype=jnp.int32)
    pos_s = PAST_LEN + jnp.arange(x_sample.shape[1], dtype=jnp.int32)
    z_ret = jnp.zeros((Bp, H_RET, DH_RET, DH_RET), f32)
    z_c = jnp.zeros((Bp, H_ML, DH_ML, DH_ML), f32)
    z_n = jnp.zeros((Bp, H_ML, DH_ML), f32)
    z_m = jnp.zeros((Bp, H_ML), f32)
    z_buf = jnp.zeros((Bp, CONV_W - 1, W_ML), x_prompt.dtype)
    lw = (g_mix[0], w_in[0], g_ret_gn[0], w_mq[0], w_mk[0], conv_w[0], conv_b[0], b_i[0], b_f[0],
          g_ml_gn[0], w_skip[0], w_up_r[0], w_up_m[0], w_out[0])
    hp, a1, a2, a3, a4, a5 = token_mix(x_prompt, pos_p, z_ret, z_c, z_n, z_m, z_buf, *lw)
    hs, b1, b2, b3, b4, b5 = token_mix(x_sample, pos_s, state_ret[0], state_mlstm_C[0],
                                       state_mlstm_n[0], state_mlstm_m[0], state_conv[0], *lw)
    n_p = x_prompt.shape[0] * x_prompt.shape[1]
    x_all = jnp.concatenate([hp.reshape(n_p, D_MODEL), hs.reshape(-1, D_MODEL)], axis=0)
    p_all = jnp.concatenate([p_prompt[0].reshape(n_p, D_PLE), p_sample[0].reshape(-1, D_PLE)], axis=0)
    x_all = channel_mix(x_all, p_all, g_ffn[0], w_pq[0], peer_keys[0], peer_u[0], peer_v[0],
                        g_ple[0], w_pg[0], w_ple[0])
    y_all = final_norm(x_all, g_final)
    y_prompt = y_all[:n_p].reshape(x_prompt.shape)
    y_sample = y_all[n_p:].reshape(x_sample.shape)
    st = lambda a: a[None]
    return (y_prompt, y_sample, st(a1), st(a2), st(a3), st(a4), st(a5),
            st(b1), st(b2), st(b3), st(b4), st(b5))
```

```python
import jax, jax.numpy as jnp
from jax import lax
import numpy as np
from jax.experimental import pallas as pl
from jax.experimental.pallas import tpu as pltpu

D_MODEL = 1024
BATCH = 4
SEQ = 8192
DEPTH = 1
DEC_BATCH = 16
DEC_SEQ = 32
PAST_LEN = 2048

CHUNK = 64
EPS = 1e-6
ROPE_BASE = 10000.0
H_RET = 4
W_RET = D_MODEL // 2
DH_RET = W_RET // H_RET
H_ML = 4
W_ML = D_MODEL // 2
DH_ML = W_ML // H_ML
CONV_W = 4
PEER_HEADS = 8
N_KEYS = 128
N_EXPERTS = N_KEYS * N_KEYS
PEER_TOPK = 16
PEER_DQ = 256
PEER_DQ_HALF = PEER_DQ // 2
PEER_BLOCK = 256
D_PLE = 256
N_IN = 4 * W_RET + 3 * W_ML + 2 * H_ML + 2 * D_MODEL


def rmsnorm(x, g):
    xf = x.astype(jnp.float32)
    y = xf * lax.rsqrt(jnp.mean(xf * xf, axis=-1, keepdims=True) + EPS)
    return (y * g.astype(jnp.float32)).astype(x.dtype)


def head_rmsnorm(y, g):
    B, T, H, d = y.shape
    y = y * lax.rsqrt(jnp.mean(y * y, axis=-1, keepdims=True) + EPS)
    return y.reshape(B, T, H * d) * g.astype(jnp.float32)


def rope(x, pos):
    half = x.shape[-1] // 2
    inv = ROPE_BASE ** (-jnp.arange(half, dtype=jnp.float32) / half)
    ang = pos.astype(jnp.float32)[:, None] * inv[None, :]
    cos = jnp.cos(ang)[None, :, None, :]
    sin = jnp.sin(ang)[None, :, None, :]
    x1, x2 = x[..., :half], x[..., half:]
    return jnp.concatenate([x1 * cos - x2 * sin, x1 * sin + x2 * cos], axis=-1)


def split_cols(z):
    sizes = (W_RET, W_RET, W_RET, W_RET, W_ML, W_ML, W_ML, H_ML, H_ML, D_MODEL, D_MODEL)
    out = []
    off = 0
    for s in sizes:
        out.append(z[..., off:off + s])
        off += s
    return out


def causal_conv(x, buf, w, b):
    T = x.shape[1]
    xp = jnp.concatenate([buf, x], axis=1)
    y = b + xp[:, 0:T] * w[0]
    for j in range(1, CONV_W):
        y = y + xp[:, j:j + T] * w[j]
    return y, xp[:, T:]


def retention_chunkwise(q, k, v, s0):
    B, T, H, dk = q.shape
    dv = v.shape[-1]
    L = min(CHUNK, T)
    nc = T // L
    to_c = lambda t: t.reshape(B, nc, L, H, t.shape[-1]).transpose(0, 3, 1, 2, 4)
    q, k, v = to_c(q), to_c(k), to_c(v)
    lg = jnp.log1p(-jnp.exp2(-5.0 - jnp.arange(H, dtype=jnp.float32)))
    idx = jnp.arange(L, dtype=jnp.float32)
    diff = idx[:, None] - idx[None, :]
    dmask = jnp.where(diff >= 0, jnp.exp(lg[:, None, None] * jnp.maximum(diff, 0.0)), 0.0)
    scores = jnp.einsum('bhcid,bhcsd->bhcis', q, k) * dmask[None, :, None]
    o = jnp.einsum('bhcis,bhcse->bhcie', scores, v)
    w_end = jnp.exp(lg[:, None] * (L - 1 - idx)[None, :])
    ds = jnp.einsum('bhcsd,bhcse->bhcde', k * w_end[None, :, None, :, None], v)
    g_chunk = jnp.exp(lg * L)[None, :, None, None]

    def step(s, ds_c):
        return g_chunk * s + ds_c, s

    s_fin, s_prev = lax.scan(step, s0, jnp.moveaxis(ds, 2, 0))
    s_prev = jnp.moveaxis(s_prev, 0, 2)
    w_read = jnp.exp(lg[:, None] * (idx + 1.0)[None, :])
    o = o + jnp.einsum('bhcid,bhcde->bhcie', q * w_read[None, :, None, :, None], s_prev)
    return o.transpose(0, 2, 3, 1, 4).reshape(B, T, H, dv), s_fin


def mlstm_chunkwise(q, k, v, ig, fg, c0, n0, m0):
    B, T, H, dk = q.shape
    L = min(CHUNK, T)
    nc = T // L
    to_c = lambda t: t.reshape(B, nc, L, H, t.shape[-1]).transpose(1, 0, 3, 2, 4)
    to_cg = lambda t: t.reshape(B, nc, L, H).transpose(1, 0, 3, 2)
    causal = jnp.tril(jnp.ones((L, L), dtype=bool))

    def step(carry, xs):
        c, n, m = carry
        qc, kc, vc, ic, lfc = xs
        F = jnp.cumsum(lfc, axis=-1)
        log_d = jnp.where(causal, ic[..., None, :] + F[..., :, None] - F[..., None, :], -jnp.inf)
        inter = m[..., None] + F
        m_t = jnp.maximum(inter, jnp.max(log_d, axis=-1))
        dw = jnp.exp(log_d - m_t[..., None])
        a = jnp.exp(inter - m_t)
        s = jnp.einsum('bhid,bhsd->bhis', qc, kc) * dw
        num = jnp.einsum('bhis,bhse->bhie', s, vc) + a[..., None] * jnp.einsum('bhid,bhde->bhie', qc, c)
        den = jnp.sum(s, axis=-1) + a * jnp.einsum('bhid,bhd->bhi', qc, n)
        hc = num / jnp.maximum(jnp.abs(den), jnp.exp(-m_t))[..., None]
        m_new = m_t[..., -1]
        w_end = jnp.exp(ic + F[..., -1:] - F - m_new[..., None])
        a_end = jnp.exp(m + F[..., -1] - m_new)
        kw = kc * w_end[..., None]
        c_new = a_end[..., None, None] * c + jnp.einsum('bhsd,bhse->bhde', kw, vc)
        n_new = a_end[..., None] * n + jnp.sum(kw, axis=-2)
        return (c_new, n_new, m_new), hc

    logf = jax.nn.log_sigmoid(fg)
    (c, n, m), h = lax.scan(step, (c0, n0, m0), (to_c(q), to_c(k), to_c(v), to_cg(ig), to_cg(logf)))
    h = h.transpose(1, 0, 3, 2, 4).reshape(B, T, H, -1)
    return h, c, n, m


SUBLANES = 8
LANES = 128
VREG_ELEMS = SUBLANES * LANES
PEER_SLOTS = PEER_HEADS * PEER_TOPK
PEER_TOK_BLOCK = 256
HALF_EXPERTS = N_EXPERTS // 2
PEER_TABLE_BYTES = HALF_EXPERTS * VREG_ELEMS * 4
PEER_VMEM_LIMIT = PEER_TABLE_BYTES + (12 << 20)
_BITREV8 = (0, 4, 2, 6, 1, 5, 3, 7)


def _sublane_transpose_sum(prods):
    sub = lax.broadcasted_iota(jnp.int32, (SUBLANES, LANES), 0)
    level = [prods[_BITREV8[i]] for i in range(8)]
    for blk, lo in ((4, sub < 4), (2, (sub & 2) == 0), (1, (sub & 1) == 0)):
        nxt = []
        for k in range(0, len(level), 2):
            a, b = level[k], level[k + 1]
            if blk == 4:
                nxt.append(jnp.where(lo, a, b) + pltpu.roll(jnp.where(lo, b, a), 4, 0))
            else:
                nxt.append(jnp.where(lo, a + pltpu.roll(a, SUBLANES - blk, 0), b + pltpu.roll(b, blk, 0)))
        level = nxt
    return level[0]


def _peer_u_kernel(idx_ref, x_ref, tab_ref, act_ref):
    lane = lax.broadcasted_iota(jnp.int32, (PEER_SLOTS, LANES), 1)
    for lg in range(PEER_TOK_BLOCK // LANES):
        cols = pl.ds(lg * LANES, LANES)
        act_ref[:, cols] = jnp.zeros((PEER_SLOTS, LANES), jnp.float32)

        def tok_body(tl, carry):
            t = lg * LANES + tl
            xt = x_ref[t]
            groups = []
            for grp in range(PEER_SLOTS // SUBLANES):
                prods = []
                for jj in range(SUBLANES):
                    row = pl.multiple_of(idx_ref[t, grp * SUBLANES + jj], SUBLANES)
                    prods.append(tab_ref[pl.ds(row, SUBLANES), :] * xt)
                groups.append(_sublane_transpose_sum(prods))
            q = jnp.concatenate(groups, axis=0)
            s = jnp.sum(q, axis=1, keepdims=True)
            act_ref[:, cols] = jnp.where(lane == tl, s, act_ref[:, cols])
            return carry

        lax.fori_loop(0, LANES, tok_body, 0, unroll=2)


def _peer_v_kernel(idx_ref, w_ref, tab_ref, out_ref):
    n_acc = 4

    def tok_body(t, carry):
        accs = [jnp.zeros((SUBLANES, LANES), jnp.float32) for _ in range(n_acc)]
        for j in range(PEER_SLOTS):
            row = pl.multiple_of(idx_ref[t, j], SUBLANES)
            accs[j % n_acc] = accs[j % n_acc] + w_ref[t, j] * tab_ref[pl.ds(row, SUBLANES), :]
        out_ref[t] = (accs[0] + accs[1]) + (accs[2] + accs[3])
        return carry

    lax.fori_loop(0, PEER_TOK_BLOCK, tok_body, 0, unroll=2)


def _peer_w_kernel(e_ref, g_ref, a0_ref, a1_ref, w0_ref, w1_ref):
    lo = e_ref[...] < HALF_EXPERTS
    act = jnp.where(lo, a0_ref[...], a1_ref[...])
    w = g_ref[...] * (0.5 * act * (1.0 + lax.erf(act * (2.0 ** -0.5))))
    w0_ref[...] = jnp.where(lo, w, 0.0)
    w1_ref[...] = jnp.where(lo, 0.0, w)


def _smem_block():
    return pl.BlockSpec((PEER_TOK_BLOCK, PEER_SLOTS), lambda i: (i, 0), memory_space=pltpu.SMEM)


def _resident_half(half):
    return pl.BlockSpec((HALF_EXPERTS * SUBLANES, LANES), lambda i: (half, 0),
                        pipeline_mode=pl.Buffered(1))


_PEER_PARAMS = pltpu.CompilerParams(dimension_semantics=("arbitrary",), vmem_limit_bytes=PEER_VMEM_LIMIT)


def _peer_pass_u(idx, x3, table, half):
    n = x3.shape[0]
    return pl.pallas_call(
        _peer_u_kernel,
        grid=(n // PEER_TOK_BLOCK,),
        in_specs=[_smem_block(),
                  pl.BlockSpec((PEER_TOK_BLOCK, SUBLANES, LANES), lambda i: (i, 0, 0)),
                  _resident_half(half)],
        out_specs=pl.BlockSpec((PEER_SLOTS, PEER_TOK_BLOCK), lambda i: (0, i)),
        out_shape=jax.ShapeDtypeStruct((PEER_SLOTS, n), jnp.float32),
        compiler_params=_PEER_PARAMS,
        name="peer_u",
    )(idx, x3, table)


def _peer_pass_v(idx, w, table, half):
    n = idx.shape[0]
    return pl.pallas_call(
        _peer_v_kernel,
        grid=(n // PEER_TOK_BLOCK,),
        in_specs=[_smem_block(), _smem_block(), _resident_half(half)],
        out_specs=pl.BlockSpec((PEER_TOK_BLOCK, SUBLANES, LANES), lambda i: (i, 0, 0)),
        out_shape=jax.ShapeDtypeStruct((n, SUBLANES, LANES), jnp.float32),
        compiler_params=_PEER_PARAMS,
        name="peer_v",
    )(idx, w, table)


def peer_gather_ffn(xn, e_t, g_t, peer_u, peer_v):
    n = xn.shape[0]
    x3 = xn.reshape(n, SUBLANES, LANES)
    u2 = peer_u.reshape(N_EXPERTS * SUBLANES, LANES)
    v2 = peer_v.reshape(N_EXPERTS * SUBLANES, LANES)
    e = e_t.T
    rows = (jnp.minimum(e, HALF_EXPERTS - 1) * SUBLANES, jnp.maximum(e - HALF_EXPERTS, 0) * SUBLANES)
    a0 = _peer_pass_u(rows[0], x3, u2, 0)
    a1 = _peer_pass_u(rows[1], x3, u2, 1)
    blk = pl.BlockSpec((PEER_SLOTS, 2 * PEER_TOK_BLOCK), lambda i: (0, i))
    w0, w1 = pl.pallas_call(
        _peer_w_kernel,
        grid=(n // (2 * PEER_TOK_BLOCK),),
        in_specs=[blk] * 4,
        out_specs=[blk] * 2,
        out_shape=[jax.ShapeDtypeStruct((PEER_SLOTS, n), jnp.float32)] * 2,
        name="peer_w",
    )(e_t, g_t, a0, a1)
    o0 = _peer_pass_v(rows[0], w0.T, v2, 0)
    o1 = _peer_pass_v(rows[1], w1.T, v2, 1)
    return o0.reshape(n, D_MODEL), o1.reshape(n, D_MODEL)


ROUTE_TOK_BLOCK = 256
NEG_INF = float("-inf")
_CAND_GROUPS = ((0, 0), (0, 8), (1, 0)) + tuple((a, 0) for a in range(2, 8))


def _take_top16_rows(s, code):
    big = jnp.float32(1e9)
    vals, codes = [], []
    for _ in range(PEER_TOPK):
        m = jnp.max(s, axis=0, keepdims=True)
        c = jnp.min(jnp.where(s == m, code, big), axis=0, keepdims=True)
        s = jnp.where(code == c, NEG_INF, s)
        vals.append(m)
        codes.append(c)
    return jnp.concatenate(vals, axis=0), jnp.concatenate(codes, axis=0)


def _lookup_rows(table, sel):
    out = jnp.zeros_like(sel)
    for a in range(PEER_TOPK):
        out = jnp.where(sel == jnp.float32(a), table[a:a + 1, :], out)
    return out


def _route_one_head(s1, s2):
    key_id = lax.broadcasted_iota(jnp.int32, (N_KEYS, LANES), 0).astype(jnp.float32)
    v1, i1 = _take_top16_rows(s1, key_id)
    v2, i2 = _take_top16_rows(s2, key_id)
    sub = lax.broadcasted_iota(jnp.int32, (SUBLANES, LANES), 0)
    subf = sub.astype(jnp.float32)
    cand, code = [], []
    for a, b0 in _CAND_GROUPS:
        c = v1[a:a + 1, :] + v2[b0:b0 + SUBLANES, :]
        n_valid = PEER_TOPK // (a + 1) - b0
        cand.append(c if n_valid >= SUBLANES else jnp.where(sub < n_valid, c, NEG_INF))
        code.append(subf + jnp.float32(a * PEER_TOPK + b0))
    cand.append(v1[SUBLANES:, :] + v2[0:1, :])
    code.append((subf + jnp.float32(SUBLANES)) * jnp.float32(PEER_TOPK))
    sc, cc = _take_top16_rows(jnp.concatenate(cand, axis=0), jnp.concatenate(code, axis=0))
    a_sel = jnp.floor(cc * jnp.float32(1.0 / PEER_TOPK))
    b_sel = cc - a_sel * jnp.float32(PEER_TOPK)
    e = _lookup_rows(i1, a_sel) * jnp.float32(N_KEYS) + _lookup_rows(i2, b_sel)
    ex = jnp.exp(sc - sc[0:1, :])
    gate = ex / jnp.sum(ex, axis=0, keepdims=True)
    return e.astype(jnp.int32), gate


def _peer_route_kernel(x_ref, g_ref, wpq_ref, keys_ref, xn_ref, e_ref, gate_ref, q_scr):
    x = x_ref[...]
    xn = x * lax.rsqrt(jnp.mean(x * x, axis=-1, keepdims=True) + EPS) * g_ref[...]
    xn_ref[...] = xn
    q = jnp.dot(xn.astype(jnp.bfloat16), wpq_ref[...], preferred_element_type=jnp.float32)
    for c in range(2 * PEER_HEADS):
        q_scr[c] = q[:, c * PEER_DQ_HALF:(c + 1) * PEER_DQ_HALF].astype(jnp.bfloat16)

    def head_body(n, carry):
        rows = pl.ds(pl.multiple_of(n * PEER_TOPK, PEER_TOPK), PEER_TOPK)
        for lg in range(ROUTE_TOK_BLOCK // LANES):
            toks = slice(lg * LANES, (lg + 1) * LANES)
            s = [lax.dot_general(keys_ref[2 * n + h], q_scr[2 * n + h, toks, :],
                                 (((1,), (1,)), ((), ())), preferred_element_type=jnp.float32)
                 for h in range(2)]
            e, gate = _route_one_head(s[0], s[1])
            e_ref[rows, toks] = e
            gate_ref[rows, toks] = gate
        return carry

    lax.fori_loop(0, PEER_HEADS, head_body, 0)


def peer_route(x, g_ffn, w_pq, peer_keys):
    n = x.shape[0]
    tok = lambda i: (i, 0)
    fixed2 = lambda i: (0, 0)
    slots = pl.BlockSpec((PEER_SLOTS, ROUTE_TOK_BLOCK), lambda i: (0, i))
    return pl.pallas_call(
        _peer_route_kernel,
        grid=(n // ROUTE_TOK_BLOCK,),
        in_specs=[pl.BlockSpec((ROUTE_TOK_BLOCK, D_MODEL), tok),
                  pl.BlockSpec((1, D_MODEL), fixed2),
                  pl.BlockSpec((D_MODEL, PEER_HEADS * PEER_DQ), fixed2),
                  pl.BlockSpec((2 * PEER_HEADS, N_KEYS, PEER_DQ_HALF), lambda i: (0, 0, 0))],
        out_specs=[pl.BlockSpec((ROUTE_TOK_BLOCK, D_MODEL), tok), slots, slots],
        out_shape=[jax.ShapeDtypeStruct((n, D_MODEL), jnp.float32),
                   jax.ShapeDtypeStruct((PEER_SLOTS, n), jnp.int32),
                   jax.ShapeDtypeStruct((PEER_SLOTS, n), jnp.float32)],
        scratch_shapes=[pltpu.VMEM((2 * PEER_HEADS, ROUTE_TOK_BLOCK, PEER_DQ_HALF), jnp.bfloat16)],
        compiler_params=pltpu.CompilerParams(dimension_semantics=("arbitrary",),
                                             vmem_limit_bytes=40 << 20),
        name="peer_route",
    )(x, g_ffn.reshape(1, D_MODEL), w_pq.astype(jnp.bfloat16),
      peer_keys.reshape(2 * PEER_HEADS, N_KEYS, PEER_DQ_HALF).astype(jnp.bfloat16))


def peer_ffn(x, g_ffn, w_pq, peer_keys, peer_u, peer_v):
    xn, e_t, g_t = peer_route(x, g_ffn, w_pq, peer_keys)
    return peer_gather_ffn(xn, e_t, g_t, peer_u, peer_v)


def token_mix(x, pos, s_ret, c_ml, n_ml, m_ml, conv_buf,
              g_mix, w_in, g_ret_gn, w_mq, w_mk, conv_w, conv_b, b_i, b_f, g_ml_gn, w_skip,
              w_up_r, w_up_m, w_out):
    B, T, _ = x.shape
    f32 = jnp.float32
    dt = x.dtype
    h = rmsnorm(x, g_mix)
    z = h @ w_in
    q_r, k_r, v_r, gt_r, xm, v_m, o_m, i_m, f_m, gate_r, gate_m = split_cols(z)
    heads = lambda t, H: t.astype(f32).reshape(B, T, H, -1)
    q_r = rope(heads(q_r, H_RET), pos)
    k_r = rope(heads(k_r, H_RET), pos) * (DH_RET ** -0.5)
    o_ret, s_ret_new = retention_chunkwise(q_r, k_r, heads(v_r, H_RET), s_ret.astype(f32))
    y_r = jax.nn.silu(gt_r.astype(f32)) * head_rmsnorm(o_ret, g_ret_gn)
    xc, conv_new = causal_conv(xm, conv_buf.astype(dt), conv_w, conv_b)
    c = jax.nn.silu(xc.astype(f32))
    ch = c.reshape(B, T, H_ML, DH_ML)
    q_m = jnp.einsum('bthd,hde->bthe', ch, w_mq.astype(f32))
    k_m = jnp.einsum('bthd,hde->bthe', ch, w_mk.astype(f32)) * (DH_ML ** -0.5)
    ig = i_m.astype(f32) + b_i.astype(f32)
    fg = f_m.astype(f32) + b_f.astype(f32)
    h_m, c_new, n_new, m_new = mlstm_chunkwise(q_m, k_m, heads(v_m, H_ML), ig, fg,
                                               c_ml.astype(f32), n_ml.astype(f32), m_ml.astype(f32))
    y_m = jax.nn.sigmoid(o_m.astype(f32)) * (head_rmsnorm(h_m, g_ml_gn) + w_skip.astype(f32) * c)
    merged = (jax.nn.sigmoid(gate_r) * (y_r.astype(dt) @ w_up_r)
              + jax.nn.sigmoid(gate_m) * (y_m.astype(dt) @ w_up_m))
    x = x + merged @ w_out
    return x, s_ret_new, c_new, n_new, m_new, conv_new


def channel_mix(x, p, g_ffn, w_pq, peer_keys, peer_u, peer_v, g_ple, w_pg, w_ple):
    x = x + peer_ffn(x, g_ffn, w_pq, peer_keys, peer_u, peer_v)
    return x + (p @ w_ple) * jax.nn.sigmoid(rmsnorm(x, g_ple) @ w_pg)


def _final_norm_kernel(x_ref, g_ref, o_ref):
    x = x_ref[...]
    o_ref[...] = x * lax.rsqrt(jnp.mean(x * x, axis=-1, keepdims=True) + EPS) * g_ref[...]


def final_norm(x, g):
    n, D = x.shape
    tm = 512
    out = pl.pallas_call(
        _final_norm_kernel,
        grid=(n // tm,),
        in_specs=[pl.BlockSpec((tm, D), lambda i: (i, 0)), pl.BlockSpec((1, D), lambda i: (0, 0))],
        out_specs=pl.BlockSpec((tm, D), lambda i: (i, 0)),
        out_shape=jax.ShapeDtypeStruct((n, D), x.dtype),
    )(x, g.reshape(1, D))
    return out


ROW_BLOCK = 256
DENSE_VMEM_LIMIT = 48 << 20
Z_RET = 4 * W_RET
Z_ML = 3 * W_ML
Z_GATE = 2 * D_MODEL
SCAN_BATCH = 4
HIGHEST = lax.Precision.HIGHEST


def _bf16(a):
    return a.astype(jnp.bfloat16)


def _dot(a, b):
    return jnp.dot(_bf16(a), _bf16(b), preferred_element_type=jnp.float32)


def _dot_nt(a, b):
    return lax.dot_general(_bf16(a), _bf16(b), (((1,), (1,)), ((), ())), preferred_element_type=jnp.float32)


def _pad_rows(a, rows):
    return jnp.concatenate([a, jnp.zeros((rows - a.shape[0], a.shape[1]), a.dtype)], axis=0)


def _dot_tn(a, b):
    return _dot(_pad_rows(a, LANES).T, _pad_rows(b, LANES))


def _rms(x, g):
    return x * lax.rsqrt(jnp.mean(x * x, axis=-1, keepdims=True) + EPS) * g


def _resident(shape):
    return pl.BlockSpec(shape, lambda *_: (0,) * len(shape), pipeline_mode=pl.Buffered(1))


def _in_proj_kernel(x_ref, g_ref, wr_ref, wm_ref, wif_ref, wg_ref, zr_ref, zm_ref, zif_ref, zg_ref):
    h = _bf16(_rms(x_ref[...], g_ref[...]))
    for w_ref, z_ref in ((wr_ref, zr_ref), (wm_ref, zm_ref), (wif_ref, zif_ref), (wg_ref, zg_ref)):
        z_ref[...] = jnp.dot(h, w_ref[...], preferred_element_type=jnp.float32)


def in_proj(x, g_mix, w_in):
    n = x.shape[0]
    w = _bf16(w_in)
    c_if = Z_RET + Z_ML
    w_if = jnp.pad(w[:, c_if:c_if + 2 * H_ML], ((0, 0), (0, LANES - 2 * H_ML)))
    widths = (Z_RET, Z_ML, LANES, Z_GATE)
    row = lambda i: (i, 0)
    return pl.pallas_call(
        _in_proj_kernel,
        grid=(n // ROW_BLOCK,),
        in_specs=[pl.BlockSpec((ROW_BLOCK, D_MODEL), row), _resident((1, D_MODEL))]
        + [_resident((D_MODEL, wd)) for wd in widths],
        out_specs=[pl.BlockSpec((ROW_BLOCK, wd), row) for wd in widths],
        out_shape=[jax.ShapeDtypeStruct((n, wd), jnp.float32) for wd in widths],
        compiler_params=pltpu.CompilerParams(dimension_semantics=("arbitrary",),
                                             vmem_limit_bytes=DENSE_VMEM_LIMIT),
        name="in_proj",
    )(x, g_mix.reshape(1, D_MODEL), w[:, :Z_RET], w[:, Z_RET:c_if], w_if, w[:, c_if + 2 * H_ML:])


def _head_norm(y):
    return y * lax.rsqrt(jnp.mean(y * y, axis=-1, keepdims=True) + EPS)


def _scan_kernel(zr_ref, zm_ref, zif_ref, cos_ref, sin_ref, s0_ref, c0_ref, n0_ref, m0_ref, cb0_ref,
                 gret_ref, wmq_ref, wmk_ref, convw_ref, convb_ref, bif_ref, gml_ref, wskip_ref,
                 yr_ref, ym_ref, sret_ref, cml_ref, nml_ref, mml_ref, ctail_ref,
                 s_scr, c_scr, n_scr, m_scr, xprev_scr):
    nb, L = zr_ref.shape[0], zr_ref.shape[1]
    chunk = pl.program_id(1)

    @pl.when(chunk == 0)
    def _():
        s_scr[...] = s0_ref[...]
        c_scr[...] = c0_ref[...]
        n_scr[...] = n0_ref[...]
        m_scr[...] = m0_ref[...]
        xprev_scr[...] = cb0_ref[...]

    f32 = jnp.float32
    row = lax.broadcasted_iota(jnp.int32, (L, L), 0)
    col = lax.broadcasted_iota(jnp.int32, (L, L), 1)
    causal = row >= col
    diff = (row - col).astype(f32)
    tril = jnp.where(causal, 1.0, 0.0).astype(f32)
    idx = lax.broadcasted_iota(jnp.int32, (L, 1), 0).astype(f32)
    cos, sin = cos_ref[...], sin_ref[...]
    half = DH_RET // 2

    for b in range(nb):
        zr = zr_ref[b]
        parts = []
        for h in range(H_RET):
            lg = float(np.log1p(-(2.0 ** (-5 - h))))
            sl = slice(h * DH_RET, (h + 1) * DH_RET)
            q = zr[:, sl]
            k = zr[:, W_RET + h * DH_RET:W_RET + (h + 1) * DH_RET]
            v = zr[:, 2 * W_RET + h * DH_RET:2 * W_RET + (h + 1) * DH_RET]
            gt = zr[:, 3 * W_RET + h * DH_RET:3 * W_RET + (h + 1) * DH_RET]
            q = q * cos + pltpu.roll(q, half, 1) * sin
            k = (k * cos + pltpu.roll(k, half, 1) * sin) * (DH_RET ** -0.5)
            dmask = jnp.where(causal, jnp.exp(lg * jnp.maximum(diff, 0.0)), 0.0)
            s_prev = s_scr[b, h]
            o = _dot(_dot_nt(q, k) * dmask, v)
            o = o + _dot(q * jnp.exp(lg * (idx + 1.0)), s_prev)
            ds = _dot_tn(k * jnp.exp(lg * (L - 1 - idx)), v)
            s_scr[b, h] = float(np.exp(lg * L)) * s_prev + ds
            parts.append(jax.nn.silu(gt) * (_head_norm(o) * gret_ref[:, sl]))
        yr_ref[b] = _bf16(jnp.concatenate(parts, axis=1))

        zm = zm_ref[b]
        xm = zm[:, :W_ML]
        xp = jnp.concatenate([xprev_scr[b], xm], axis=0)
        first = SUBLANES - (CONV_W - 1)
        xc = convb_ref[...] + xp[first:first + L] * convw_ref[0:1, :]
        for j in range(1, CONV_W):
            xc = xc + xp[first + j:first + j + L] * convw_ref[j:j + 1, :]
        xprev_scr[b] = xm[L - SUBLANES:, :]
        cact = jax.nn.silu(xc)
        gates = zif_ref[b] + bif_ref[...]
        fcum = jnp.dot(tril, jax.nn.log_sigmoid(gates), precision=HIGHEST, preferred_element_type=f32)
        gates_t = _pad_rows(gates, LANES).T
        fcum_t = _pad_rows(fcum, LANES).T
        parts = []
        for h in range(H_ML):
            sl = slice(h * DH_ML, (h + 1) * DH_ML)
            ch = cact[:, sl]
            q = _dot(ch, wmq_ref[h])
            k = _dot(ch, wmk_ref[h]) * (DH_ML ** -0.5)
            v = zm[:, W_ML + h * DH_ML:W_ML + (h + 1) * DH_ML]
            o_gate = zm[:, 2 * W_ML + h * DH_ML:2 * W_ML + (h + 1) * DH_ML]
            ig_row, f_row = gates_t[h:h + 1, :L], fcum_t[H_ML + h:H_ML + h + 1, :L]
            ig_col, f_col = gates[:, h:h + 1], fcum[:, H_ML + h:H_ML + h + 1]
            m_prev = m_scr[b, h:h + 1, 0:1]
            c_prev = c_scr[b, h]
            n_prev = n_scr[b, h:h + 1, :]
            log_d = jnp.where(causal, ig_row + f_col - f_row, NEG_INF)
            inter = m_prev + f_col
            m_t = jnp.maximum(inter, jnp.max(log_d, axis=1, keepdims=True))
            a = jnp.exp(inter - m_t)
            s = _dot_nt(q, k) * jnp.exp(log_d - m_t)
            num = _dot(s, v) + a * _dot(q, c_prev)
            den = jnp.sum(s, axis=1, keepdims=True) + a * jnp.sum(q * n_prev, axis=1, keepdims=True)
            hc = num / jnp.maximum(jnp.abs(den), jnp.exp(-m_t))
            m_new = m_t[L - 1:L, :]
            f_last = f_col[L - 1:L, :]
            a_end = jnp.exp(m_prev + f_last - m_new)
            kw = k * jnp.exp(ig_col + f_last - f_col - m_new)
            c_scr[b, h] = a_end * c_prev + _dot_tn(kw, v)
            n_scr[b, h:h + 1, :] = a_end * n_prev + jnp.sum(kw, axis=0, keepdims=True)
            m_scr[b, h:h + 1, :] = jnp.broadcast_to(m_new, (1, LANES))
            parts.append(jax.nn.sigmoid(o_gate) * (_head_norm(hc) * gml_ref[:, sl] + wskip_ref[:, sl] * ch))
        ym_ref[b] = _bf16(jnp.concatenate(parts, axis=1))

    @pl.when(chunk == pl.num_programs(1) - 1)
    def _():
        sret_ref[...] = s_scr[...]
        cml_ref[...] = c_scr[...]
        nml_ref[...] = n_scr[...]
        mml_ref[...] = m_scr[...]
        ctail_ref[...] = xprev_scr[...]


def _rope_tables(pos):
    inv = ROPE_BASE ** (-jnp.arange(DH_RET // 2, dtype=jnp.float32) / (DH_RET // 2))
    ang = pos.astype(jnp.float32)[:, None] * inv[None, :]
    cos, sin = jnp.cos(ang), jnp.sin(ang)
    return jnp.concatenate([cos, cos], axis=1), jnp.concatenate([-sin, sin], axis=1)


def token_mixers(zr, zm, zif, pos, s0, c0, n0, m0, conv_buf,
                 g_ret_gn, w_mq, w_mk, conv_w, conv_b, b_i, b_f, g_ml_gn, w_skip):
    B, T = zr.shape[0], zr.shape[1]
    L = min(CHUNK, T)
    nb = SCAN_BATCH
    cos2, sin2 = _rope_tables(pos)
    f32 = jnp.float32
    m0b = jnp.broadcast_to(m0.astype(f32)[:, :, None], (B, H_ML, LANES))
    cb0 = jnp.pad(conv_buf.astype(f32), ((0, 0), (SUBLANES - (CONV_W - 1), 0), (0, 0)))
    bif = jnp.pad(jnp.concatenate([b_i, b_f]).astype(f32), (0, LANES - 2 * H_ML)).reshape(1, LANES)
    seq = lambda wd: pl.BlockSpec((nb, L, wd), lambda i, c: (i, c, 0))
    tab = pl.BlockSpec((L, DH_RET), lambda i, c: (c, 0))
    st4 = pl.BlockSpec((nb, H_RET, DH_RET, DH_RET), lambda i, c: (i, 0, 0, 0))
    st3 = pl.BlockSpec((nb, H_ML, DH_ML), lambda i, c: (i, 0, 0))
    cb = pl.BlockSpec((nb, SUBLANES, W_ML), lambda i, c: (i, 0, 0))
    row512 = _resident((1, W_ML))
    outs = pl.pallas_call(
        _scan_kernel,
        grid=(B // nb, T // L),
        in_specs=[seq(Z_RET), seq(Z_ML), seq(LANES), tab, tab, st4, st4, st3, st3, cb,
                  row512, _resident((H_ML, DH_ML, DH_ML)), _resident((H_ML, DH_ML, DH_ML)),
                  _resident((CONV_W, W_ML)), row512, _resident((1, LANES)), row512, row512],
        out_specs=[seq(W_RET), seq(W_ML), st4, st4, st3, st3, cb],
        out_shape=[jax.ShapeDtypeStruct((B, T, W_RET), jnp.bfloat16),
                   jax.ShapeDtypeStruct((B, T, W_ML), jnp.bfloat16),
                   jax.ShapeDtypeStruct((B, H_RET, DH_RET, DH_RET), f32),
                   jax.ShapeDtypeStruct((B, H_ML, DH_ML, DH_ML), f32),
                   jax.ShapeDtypeStruct((B, H_ML, DH_ML), f32),
                   jax.ShapeDtypeStruct((B, H_ML, LANES), f32),
                   jax.ShapeDtypeStruct((B, SUBLANES, W_ML), f32)],
        scratch_shapes=[pltpu.VMEM((nb, H_RET, DH_RET, DH_RET), f32),
                        pltpu.VMEM((nb, H_ML, DH_ML, DH_ML), f32),
                        pltpu.VMEM((nb, H_ML, DH_ML), f32),
                        pltpu.VMEM((nb, H_ML, LANES), f32),
                        pltpu.VMEM((nb, SUBLANES, W_ML), f32)],
        compiler_params=pltpu.CompilerParams(dimension_semantics=("arbitrary", "arbitrary"),
                                             vmem_limit_bytes=DENSE_VMEM_LIMIT),
        name="token_mixers",
    )(zr, zm, zif, cos2, sin2, s0.astype(f32), c0.astype(f32), n0.astype(f32), m0b, cb0,
      g_ret_gn.reshape(1, W_RET), _bf16(w_mq), _bf16(w_mk), conv_w, conv_b.reshape(1, W_ML), bif,
      g_ml_gn.reshape(1, W_ML), w_skip.reshape(1, W_ML))
    y_r, y_m, s_new, c_new, n_new, m_new, ctail = outs
    return y_r, y_m, s_new, c_new, n_new, m_new[:, :, 0], ctail[:, SUBLANES - (CONV_W - 1):, :]


def _merge_kernel(x_ref, yr_ref, ym_ref, zg_ref, wur_ref, wum_ref, wo_ref, o_ref):
    zg = zg_ref[...]
    up_r = jnp.dot(yr_ref[...], wur_ref[...], preferred_element_type=jnp.float32)
    up_m = jnp.dot(ym_ref[...], wum_ref[...], preferred_element_type=jnp.float32)
    merged = jax.nn.sigmoid(zg[:, :D_MODEL]) * up_r + jax.nn.sigmoid(zg[:, D_MODEL:]) * up_m
    o_ref[...] = x_ref[...] + jnp.dot(_bf16(merged), wo_ref[...], preferred_element_type=jnp.float32)


def merge_branches(x, y_r, y_m, zg, w_up_r, w_up_m, w_out):
    n = x.shape[0]
    row = lambda i: (i, 0)
    return pl.pallas_call(
        _merge_kernel,
        grid=(n // ROW_BLOCK,),
        in_specs=[pl.BlockSpec((ROW_BLOCK, D_MODEL), row), pl.BlockSpec((ROW_BLOCK, W_RET), row),
                  pl.BlockSpec((ROW_BLOCK, W_ML), row), pl.BlockSpec((ROW_BLOCK, Z_GATE), row),
                  _resident((W_RET, D_MODEL)), _resident((W_ML, D_MODEL)), _resident((D_MODEL, D_MODEL))],
        out_specs=pl.BlockSpec((ROW_BLOCK, D_MODEL), row),
        out_shape=jax.ShapeDtypeStruct((n, D_MODEL), jnp.float32),
        compiler_params=pltpu.CompilerParams(dimension_semantics=("arbitrary",),
                                             vmem_limit_bytes=DENSE_VMEM_LIMIT),
        name="merge_branches",
    )(x, y_r, y_m, zg, _bf16(w_up_r), _bf16(w_up_m), _bf16(w_out))


def _ple_kernel(x_ref, o0_ref, o1_ref, p_ref, gple_ref, wpg_ref, wple_ref, gfin_ref, y_ref):
    x = x_ref[...] + (o0_ref[...] + o1_ref[...])
    gate = jax.nn.sigmoid(_dot(_rms(x, gple_ref[...]), wpg_ref[...]))
    x = x + _dot(p_ref[...], wple_ref[...]) * gate
    y_ref[...] = _rms(x, gfin_ref[...])


def ple_and_final_norm(x, o0, o1, p, g_ple, w_pg, w_ple, g_final):
    n = x.shape[0]
    row = lambda i: (i, 0)
    blk = pl.BlockSpec((ROW_BLOCK, D_MODEL), row)
    return pl.pallas_call(
        _ple_kernel,
        grid=(n // ROW_BLOCK,),
        in_specs=[blk, blk, blk, pl.BlockSpec((ROW_BLOCK, D_PLE), row), _resident((1, D_MODEL)),
                  _resident((D_MODEL, D_MODEL)), _resident((D_PLE, D_MODEL)), _resident((1, D_MODEL))],
        out_specs=blk,
        out_shape=jax.ShapeDtypeStruct((n, D_MODEL), jnp.float32),
        compiler_params=pltpu.CompilerParams(dimension_semantics=("arbitrary",),
                                             vmem_limit_bytes=DENSE_VMEM_LIMIT),
        name="ple_final",
    )(x, o0, o1, p, g_ple.reshape(1, D_MODEL), _bf16(w_pg), _bf16(w_ple), g_final.reshape(1, D_MODEL))


def kernel(x_prompt, x_sample, p_prompt, p_sample, state_ret, state_mlstm_C, state_mlstm_n,
           state_mlstm_m, state_conv, g_mix, w_in, g_ret_gn, w_mq, w_mk, conv_w, conv_b, b_i, b_f,
           g_ml_gn, w_skip, w_up_r, w_up_m, w_out, g_ffn, w_pq, peer_keys, peer_u, peer_v,
           g_ple, w_pg, w_ple, g_final):
    f32 = jnp.float32
    Bp = x_prompt.shape[0]
    pos_p = jnp.arange(x_prompt.shape[1], dtype=jnp.int32)
    pos_s = PAST_LEN + jnp.arange(x_sample.shape[1], dtype=jnp.int32)
    z_ret = jnp.zeros((Bp, H_RET, DH_RET, DH_RET), f32)
    z_c = jnp.zeros((Bp, H_ML, DH_ML, DH_ML), f32)
    z_n = jnp.zeros((Bp, H_ML, DH_ML), f32)
    z_m = jnp.zeros((Bp, H_ML), f32)
    z_buf = jnp.zeros((Bp, CONV_W - 1, W_ML), x_prompt.dtype)
    mixer_w = (g_ret_gn[0], w_mq[0], w_mk[0], conv_w[0], conv_b[0], b_i[0], b_f[0], g_ml_gn[0], w_skip[0])

    def stream(x, pos, s0, c0, n0, m0, cbuf):
        B, T, _ = x.shape
        x2 = x.reshape(B * T, D_MODEL)
        zr, zm, zif, zg = in_proj(x2, g_mix[0], w_in[0])
        seq = lambda z: z.reshape(B, T, z.shape[-1])
        y_r, y_m, *states = token_mixers(seq(zr), seq(zm), seq(zif), pos, s0, c0, n0, m0, cbuf, *mixer_w)
        x1 = merge_branches(x2, y_r.reshape(B * T, W_RET), y_m.reshape(B * T, W_ML), zg,
                            w_up_r[0], w_up_m[0], w_out[0])
        return x1, states

    hp, st_p = stream(x_prompt, pos_p, z_ret, z_c, z_n, z_m, z_buf)
    hs, st_s = stream(x_sample, pos_s, state_ret[0], state_mlstm_C[0], state_mlstm_n[0],
                      state_mlstm_m[0], state_conv[0])
    n_p = hp.shape[0]
    x_all = jnp.concatenate([hp, hs], axis=0)
    p_all = jnp.concatenate([p_prompt[0].reshape(n_p, D_PLE), p_sample[0].reshape(-1, D_PLE)], axis=0)
    o0, o1 = peer_ffn(x_all, g_ffn[0], w_pq[0], peer_keys[0], peer_u[0], peer_v[0])
    y_all = ple_and_final_norm(x_all, o0, o1, p_all, g_ple[0], w_pg[0], w_ple[0], g_final)
    y_prompt = y_all[:n_p].reshape(x_prompt.shape)
    y_sample = y_all[n_p:].reshape(x_sample.shape)
    return (y_prompt, y_sample, *[a[None] for a in st_p], *[a[None] for a in st_s])
```

```python
import jax, jax.numpy as jnp
from jax import lax
import numpy as np
from jax.experimental import pallas as pl
from jax.experimental.pallas import tpu as pltpu

D_MODEL = 1024
PAST_LEN = 2048
CHUNK = 64
EPS = 1e-6
ROPE_BASE = 10000.0
H_RET = 4
W_RET = D_MODEL // 2
DH_RET = W_RET // H_RET
H_ML = 4
W_ML = D_MODEL // 2
DH_ML = W_ML // H_ML
CONV_W = 4
PEER_HEADS = 8
N_KEYS = 128
N_EXPERTS = N_KEYS * N_KEYS
PEER_TOPK = 16
PEER_DQ = 256
PEER_DQ_HALF = PEER_DQ // 2
D_PLE = 256

SUBLANES = 8
LANES = 128
NEG_INF = float("-inf")
HIGHEST = lax.Precision.HIGHEST
NT_DIMS = (((1,), (1,)), ((), ()))


def _bf16(a):
    return a.astype(jnp.bfloat16)


def _dot(a, b):
    return jnp.dot(_bf16(a), _bf16(b), preferred_element_type=jnp.float32)


def _dot_nt(a, b):
    return lax.dot_general(_bf16(a), _bf16(b), NT_DIMS, preferred_element_type=jnp.float32)


def _pad_rows(a, rows):
    return jnp.concatenate([a, jnp.zeros((rows - a.shape[0], a.shape[1]), a.dtype)], axis=0)


def _dot_tn(a, b):
    return _dot(_pad_rows(a, LANES).T, _pad_rows(b, LANES))


def _rms(x, g):
    return x * lax.rsqrt(jnp.mean(x * x, axis=-1, keepdims=True) + EPS) * g


def _resident(shape):
    return pl.BlockSpec(shape, lambda *_: (0,) * len(shape), pipeline_mode=pl.Buffered(1))


ROW_BLOCK = 256
DENSE_VMEM_LIMIT = 48 << 20
Z_RET = 4 * W_RET
Z_ML = 3 * W_ML
Z_GATE = 2 * D_MODEL
SCAN_BATCH = 4


def _in_proj_kernel(x_ref, g_ref, wr_ref, wm_ref, wif_ref, wg_ref, zr_ref, zm_ref, zif_ref, zg_ref):
    h = _bf16(_rms(x_ref[...], g_ref[...]))
    for w_ref, z_ref in ((wr_ref, zr_ref), (wm_ref, zm_ref), (wif_ref, zif_ref), (wg_ref, zg_ref)):
        z_ref[...] = jnp.dot(h, w_ref[...], preferred_element_type=jnp.float32)


def in_proj(x, g_mix, w_in):
    n = x.shape[0]
    w = _bf16(w_in)
    c_if = Z_RET + Z_ML
    w_if = jnp.pad(w[:, c_if:c_if + 2 * H_ML], ((0, 0), (0, LANES - 2 * H_ML)))
    widths = (Z_RET, Z_ML, LANES, Z_GATE)
    row = lambda i: (i, 0)
    return pl.pallas_call(
        _in_proj_kernel,
        grid=(n // ROW_BLOCK,),
        in_specs=[pl.BlockSpec((ROW_BLOCK, D_MODEL), row), _resident((1, D_MODEL))]
        + [_resident((D_MODEL, wd)) for wd in widths],
        out_specs=[pl.BlockSpec((ROW_BLOCK, wd), row) for wd in widths],
        out_shape=[jax.ShapeDtypeStruct((n, wd), jnp.float32) for wd in widths],
        compiler_params=pltpu.CompilerParams(dimension_semantics=("arbitrary",),
                                             vmem_limit_bytes=DENSE_VMEM_LIMIT),
        name="in_proj",
    )(x, g_mix.reshape(1, D_MODEL), w[:, :Z_RET], w[:, Z_RET:c_if], w_if, w[:, c_if + 2 * H_ML:])


def _head_norm(y):
    return y * lax.rsqrt(jnp.mean(y * y, axis=-1, keepdims=True) + EPS)


def _scan_kernel(zr_ref, zm_ref, zif_ref, cos_ref, sin_ref, s0_ref, c0_ref, n0_ref, m0_ref, cb0_ref,
                 gret_ref, wmq_ref, wmk_ref, convw_ref, convb_ref, bif_ref, gml_ref, wskip_ref,
                 yr_ref, ym_ref, sret_ref, cml_ref, nml_ref, mml_ref, ctail_ref,
                 s_scr, c_scr, n_scr, m_scr, xprev_scr):
    nb, L = zr_ref.shape[0], zr_ref.shape[1]
    chunk = pl.program_id(1)

    @pl.when(chunk == 0)
    def _():
        s_scr[...] = s0_ref[...]
        c_scr[...] = c0_ref[...]
        n_scr[...] = n0_ref[...]
        m_scr[...] = m0_ref[...]
        xprev_scr[...] = cb0_ref[...]

    f32 = jnp.float32
    row = lax.broadcasted_iota(jnp.int32, (L, L), 0)
    col = lax.broadcasted_iota(jnp.int32, (L, L), 1)
    causal = row >= col
    diff = (row - col).astype(f32)
    tril = jnp.where(causal, 1.0, 0.0).astype(f32)
    idx = lax.broadcasted_iota(jnp.int32, (L, 1), 0).astype(f32)
    cos, sin = cos_ref[...], sin_ref[...]
    half = DH_RET // 2

    for b in range(nb):
        zr = zr_ref[b]
        parts = []
        for h in range(H_RET):
            lg = float(np.log1p(-(2.0 ** (-5 - h))))
            sl = slice(h * DH_RET, (h + 1) * DH_RET)
            q = zr[:, sl]
            k = zr[:, W_RET + h * DH_RET:W_RET + (h + 1) * DH_RET]
            v = zr[:, 2 * W_RET + h * DH_RET:2 * W_RET + (h + 1) * DH_RET]
            gt = zr[:, 3 * W_RET + h * DH_RET:3 * W_RET + (h + 1) * DH_RET]
            q = q * cos + pltpu.roll(q, half, 1) * sin
            k = (k * cos + pltpu.roll(k, half, 1) * sin) * (DH_RET ** -0.5)
            dmask = jnp.where(causal, jnp.exp(lg * jnp.maximum(diff, 0.0)), 0.0)
            s_prev = s_scr[b, h]
            o = _dot(_dot_nt(q, k) * dmask, v)
            o = o + _dot(q * jnp.exp(lg * (idx + 1.0)), s_prev)
            ds = _dot_tn(k * jnp.exp(lg * (L - 1 - idx)), v)
            s_scr[b, h] = float(np.exp(lg * L)) * s_prev + ds
            parts.append(jax.nn.silu(gt) * (_head_norm(o) * gret_ref[:, sl]))
        yr_ref[b] = _bf16(jnp.concatenate(parts, axis=1))

        zm = zm_ref[b]
        xm = zm[:, :W_ML]
        xp = jnp.concatenate([xprev_scr[b], xm], axis=0)
        first = SUBLANES - (CONV_W - 1)
        xc = convb_ref[...] + xp[first:first + L] * convw_ref[0:1, :]
        for j in range(1, CONV_W):
            xc = xc + xp[first + j:first + j + L] * convw_ref[j:j + 1, :]
        xprev_scr[b] = xm[L - SUBLANES:, :]
        cact = jax.nn.silu(xc)
        gates = zif_ref[b] + bif_ref[...]
        fcum = jnp.dot(tril, jax.nn.log_sigmoid(gates), precision=HIGHEST, preferred_element_type=f32)
        gates_t = _pad_rows(gates, LANES).T
        fcum_t = _pad_rows(fcum, LANES).T
        parts = []
        for h in range(H_ML):
            sl = slice(h * DH_ML, (h + 1) * DH_ML)
            ch = cact[:, sl]
            q = _dot(ch, wmq_ref[h])
            k = _dot(ch, wmk_ref[h]) * (DH_ML ** -0.5)
            v = zm[:, W_ML + h * DH_ML:W_ML + (h + 1) * DH_ML]
            o_gate = zm[:, 2 * W_ML + h * DH_ML:2 * W_ML + (h + 1) * DH_ML]
            ig_row, f_row = gates_t[h:h + 1, :L], fcum_t[H_ML + h:H_ML + h + 1, :L]
            ig_col, f_col = gates[:, h:h + 1], fcum[:, H_ML + h:H_ML + h + 1]
            m_prev = m_scr[b, h:h + 1, 0:1]
            c_prev = c_scr[b, h]
            n_prev = n_scr[b, h:h + 1, :]
            log_d = jnp.where(causal, ig_row + f_col - f_row, NEG_INF)
            inter = m_prev + f_col
            m_t = jnp.maximum(inter, jnp.max(log_d, axis=1, keepdims=True))
            a = jnp.exp(inter - m_t)
            s = _dot_nt(q, k) * jnp.exp(log_d - m_t)
            num = _dot(s, v) + a * _dot(q, c_prev)
            den = jnp.sum(s, axis=1, keepdims=True) + a * jnp.sum(q * n_prev, axis=1, keepdims=True)
            hc = num / jnp.maximum(jnp.abs(den), jnp.exp(-m_t))
            m_new = m_t[L - 1:L, :]
            f_last = f_col[L - 1:L, :]
            a_end = jnp.exp(m_prev + f_last - m_new)
            kw = k * jnp.exp(ig_col + f_last - f_col - m_new)
            c_scr[b, h] = a_end * c_prev + _dot_tn(kw, v)
            n_scr[b, h:h + 1, :] = a_end * n_prev + jnp.sum(kw, axis=0, keepdims=True)
            m_scr[b, h:h + 1, :] = jnp.broadcast_to(m_new, (1, LANES))
            parts.append(jax.nn.sigmoid(o_gate) * (_head_norm(hc) * gml_ref[:, sl] + wskip_ref[:, sl] * ch))
        ym_ref[b] = _bf16(jnp.concatenate(parts, axis=1))

    @pl.when(chunk == pl.num_programs(1) - 1)
    def _():
        sret_ref[...] = s_scr[...]
        cml_ref[...] = c_scr[...]
        nml_ref[...] = n_scr[...]
        mml_ref[...] = m_scr[...]
        ctail_ref[...] = xprev_scr[...]


def _rope_tables(pos):
    inv = ROPE_BASE ** (-jnp.arange(DH_RET // 2, dtype=jnp.float32) / (DH_RET // 2))
    ang = pos.astype(jnp.float32)[:, None] * inv[None, :]
    cos, sin = jnp.cos(ang), jnp.sin(ang)
    return jnp.concatenate([cos, cos], axis=1), jnp.concatenate([-sin, sin], axis=1)


def token_mixers(zr, zm, zif, pos, s0, c0, n0, m0, conv_buf,
                 g_ret_gn, w_mq, w_mk, conv_w, conv_b, b_i, b_f, g_ml_gn, w_skip):
    B, T = zr.shape[0], zr.shape[1]
    L = min(CHUNK, T)
    nb = SCAN_BATCH
    cos2, sin2 = _rope_tables(pos)
    f32 = jnp.float32
    m0b = jnp.broadcast_to(m0.astype(f32)[:, :, None], (B, H_ML, LANES))
    cb0 = jnp.pad(conv_buf.astype(f32), ((0, 0), (SUBLANES - (CONV_W - 1), 0), (0, 0)))
    bif = jnp.pad(jnp.concatenate([b_i, b_f]).astype(f32), (0, LANES - 2 * H_ML)).reshape(1, LANES)
    seq = lambda wd: pl.BlockSpec((nb, L, wd), lambda i, c: (i, c, 0))
    tab = pl.BlockSpec((L, DH_RET), lambda i, c: (c, 0))
    st4 = pl.BlockSpec((nb, H_RET, DH_RET, DH_RET), lambda i, c: (i, 0, 0, 0))
    st3 = pl.BlockSpec((nb, H_ML, DH_ML), lambda i, c: (i, 0, 0))
    cb = pl.BlockSpec((nb, SUBLANES, W_ML), lambda i, c: (i, 0, 0))
    row512 = _resident((1, W_ML))
    outs = pl.pallas_call(
        _scan_kernel,
        grid=(B // nb, T // L),
        in_specs=[seq(Z_RET), seq(Z_ML), seq(LANES), tab, tab, st4, st4, st3, st3, cb,
                  row512, _resident((H_ML, DH_ML, DH_ML)), _resident((H_ML, DH_ML, DH_ML)),
                  _resident((CONV_W, W_ML)), row512, _resident((1, LANES)), row512, row512],
        out_specs=[seq(W_RET), seq(W_ML), st4, st4, st3, st3, cb],
        out_shape=[jax.ShapeDtypeStruct((B, T, W_RET), jnp.bfloat16),
                   jax.ShapeDtypeStruct((B, T, W_ML), jnp.bfloat16),
                   jax.ShapeDtypeStruct((B, H_RET, DH_RET, DH_RET), f32),
                   jax.ShapeDtypeStruct((B, H_ML, DH_ML, DH_ML), f32),
                   jax.ShapeDtypeStruct((B, H_ML, DH_ML), f32),
                   jax.ShapeDtypeStruct((B, H_ML, LANES), f32),
                   jax.ShapeDtypeStruct((B, SUBLANES, W_ML), f32)],
        scratch_shapes=[pltpu.VMEM((nb, H_RET, DH_RET, DH_RET), f32),
                        pltpu.VMEM((nb, H_ML, DH_ML, DH_ML), f32),
                        pltpu.VMEM((nb, H_ML, DH_ML), f32),
                        pltpu.VMEM((nb, H_ML, LANES), f32),
                        pltpu.VMEM((nb, SUBLANES, W_ML), f32)],
        compiler_params=pltpu.CompilerParams(dimension_semantics=("arbitrary", "arbitrary"),
                                             vmem_limit_bytes=DENSE_VMEM_LIMIT),
        name="token_mixers",
    )(zr, zm, zif, cos2, sin2, s0.astype(f32), c0.astype(f32), n0.astype(f32), m0b, cb0,
      g_ret_gn.reshape(1, W_RET), _bf16(w_mq), _bf16(w_mk), conv_w, conv_b.reshape(1, W_ML), bif,
      g_ml_gn.reshape(1, W_ML), w_skip.reshape(1, W_ML))
    y_r, y_m, s_new, c_new, n_new, m_new, ctail = outs
    return y_r, y_m, s_new, c_new, n_new, m_new[:, :, 0], ctail[:, SUBLANES - (CONV_W - 1):, :]


def _merge_kernel(x_ref, yr_ref, ym_ref, zg_ref, wur_ref, wum_ref, wo_ref, o_ref):
    zg = zg_ref[...]
    up_r = jnp.dot(yr_ref[...], wur_ref[...], preferred_element_type=jnp.float32)
    up_m = jnp.dot(ym_ref[...], wum_ref[...], preferred_element_type=jnp.float32)
    merged = jax.nn.sigmoid(zg[:, :D_MODEL]) * up_r + jax.nn.sigmoid(zg[:, D_MODEL:]) * up_m
    o_ref[...] = x_ref[...] + jnp.dot(_bf16(merged), wo_ref[...], preferred_element_type=jnp.float32)


def merge_branches(x, y_r, y_m, zg, w_up_r, w_up_m, w_out):
    n = x.shape[0]
    row = lambda i: (i, 0)
    return pl.pallas_call(
        _merge_kernel,
        grid=(n // ROW_BLOCK,),
        in_specs=[pl.BlockSpec((ROW_BLOCK, D_MODEL), row), pl.BlockSpec((ROW_BLOCK, W_RET), row),
                  pl.BlockSpec((ROW_BLOCK, W_ML), row), pl.BlockSpec((ROW_BLOCK, Z_GATE), row),
                  _resident((W_RET, D_MODEL)), _resident((W_ML, D_MODEL)), _resident((D_MODEL, D_MODEL))],
        out_specs=pl.BlockSpec((ROW_BLOCK, D_MODEL), row),
        out_shape=jax.ShapeDtypeStruct((n, D_MODEL), jnp.float32),
        compiler_params=pltpu.CompilerParams(dimension_semantics=("arbitrary",),
                                             vmem_limit_bytes=DENSE_VMEM_LIMIT),
        name="merge_branches",
    )(x, y_r, y_m, zg, _bf16(w_up_r), _bf16(w_up_m), _bf16(w_out))


PEER_SLOTS = PEER_HEADS * PEER_TOPK
ROUTE_TOK_BLOCK = 256
_CAND_GROUPS = ((0, 0), (0, 8), (1, 0)) + tuple((a, 0) for a in range(2, 8))


def _take_top16_rows(s, code):
    big = jnp.float32(1e9)
    vals, codes = [], []
    for _ in range(PEER_TOPK):
        m = jnp.max(s, axis=0, keepdims=True)
        c = jnp.min(jnp.where(s == m, code, big), axis=0, keepdims=True)
        s = jnp.where(code == c, NEG_INF, s)
        vals.append(m)
        codes.append(c)
    return jnp.concatenate(vals, axis=0), jnp.concatenate(codes, axis=0)


def _lookup_rows(table, sel):
    out = jnp.zeros_like(sel)
    for a in range(PEER_TOPK):
        out = jnp.where(sel == jnp.float32(a), table[a:a + 1, :], out)
    return out


def _route_one_head(s1, s2):
    key_id = lax.broadcasted_iota(jnp.int32, (N_KEYS, LANES), 0).astype(jnp.float32)
    v1, i1 = _take_top16_rows(s1, key_id)
    v2, i2 = _take_top16_rows(s2, key_id)
    sub = lax.broadcasted_iota(jnp.int32, (SUBLANES, LANES), 0)
    subf = sub.astype(jnp.float32)
    cand, code = [], []
    for a, b0 in _CAND_GROUPS:
        c = v1[a:a + 1, :] + v2[b0:b0 + SUBLANES, :]
        n_valid = PEER_TOPK // (a + 1) - b0
        cand.append(c if n_valid >= SUBLANES else jnp.where(sub < n_valid, c, NEG_INF))
        code.append(subf + jnp.float32(a * PEER_TOPK + b0))
    cand.append(v1[SUBLANES:, :] + v2[0:1, :])
    code.append((subf + jnp.float32(SUBLANES)) * jnp.float32(PEER_TOPK))
    sc, cc = _take_top16_rows(jnp.concatenate(cand, axis=0), jnp.concatenate(code, axis=0))
    a_sel = jnp.floor(cc * jnp.float32(1.0 / PEER_TOPK))
    b_sel = cc - a_sel * jnp.float32(PEER_TOPK)
    e = _lookup_rows(i1, a_sel) * jnp.float32(N_KEYS) + _lookup_rows(i2, b_sel)
    ex = jnp.exp(sc - sc[0:1, :])
    gate = ex / jnp.sum(ex, axis=0, keepdims=True)
    return e.astype(jnp.int32), gate


def _peer_route_kernel(x_ref, g_ref, wpq_ref, keys_ref, xn_ref, e_ref, gate_ref, q_scr):
    xn = _rms(x_ref[...], g_ref[...])
    xn_ref[...] = xn
    q = jnp.dot(_bf16(xn), wpq_ref[...], preferred_element_type=jnp.float32)
    for c in range(2 * PEER_HEADS):
        q_scr[c] = _bf16(q[:, c * PEER_DQ_HALF:(c + 1) * PEER_DQ_HALF])

    def head_body(n, carry):
        rows = pl.ds(pl.multiple_of(n * PEER_TOPK, PEER_TOPK), PEER_TOPK)
        for lg in range(ROUTE_TOK_BLOCK // LANES):
            toks = slice(lg * LANES, (lg + 1) * LANES)
            s = [lax.dot_general(keys_ref[2 * n + h], q_scr[2 * n + h, toks, :], NT_DIMS,
                                 preferred_element_type=jnp.float32) for h in range(2)]
            e, gate = _route_one_head(s[0], s[1])
            e_ref[rows, toks] = e
            gate_ref[rows, toks] = gate
        return carry

    lax.fori_loop(0, PEER_HEADS, head_body, 0)


def peer_route(x, g_ffn, w_pq, peer_keys):
    n = x.shape[0]
    tok = lambda i: (i, 0)
    slots = pl.BlockSpec((PEER_SLOTS, ROUTE_TOK_BLOCK), lambda i: (0, i))
    return pl.pallas_call(
        _peer_route_kernel,
        grid=(n // ROUTE_TOK_BLOCK,),
        in_specs=[pl.BlockSpec((ROUTE_TOK_BLOCK, D_MODEL), tok),
                  _resident((1, D_MODEL)),
                  _resident((D_MODEL, PEER_HEADS * PEER_DQ)),
                  _resident((2 * PEER_HEADS, N_KEYS, PEER_DQ_HALF))],
        out_specs=[pl.BlockSpec((ROUTE_TOK_BLOCK, D_MODEL), tok), slots, slots],
        out_shape=[jax.ShapeDtypeStruct((n, D_MODEL), jnp.float32),
                   jax.ShapeDtypeStruct((PEER_SLOTS, n), jnp.int32),
                   jax.ShapeDtypeStruct((PEER_SLOTS, n), jnp.float32)],
        scratch_shapes=[pltpu.VMEM((2 * PEER_HEADS, ROUTE_TOK_BLOCK, PEER_DQ_HALF), jnp.bfloat16)],
        compiler_params=pltpu.CompilerParams(dimension_semantics=("arbitrary",),
                                             vmem_limit_bytes=DENSE_VMEM_LIMIT),
        name="peer_route",
    )(x, g_ffn.reshape(1, D_MODEL), _bf16(w_pq),
      _bf16(peer_keys.reshape(2 * PEER_HEADS, N_KEYS, PEER_DQ_HALF)))


PAIRED_EXPERTS = N_EXPERTS // 2
TILE_ROWS = 2 * SUBLANES
TILE_ROWS_LOG2 = TILE_ROWS.bit_length() - 1
PEER_K = PEER_SLOTS * TILE_ROWS
PEER_TOK_BLOCK = 256
PEER_TOK_UNROLL = 2
PACK_ROW_BLOCK = 2048
PEER_TABLE_BYTES = PAIRED_EXPERTS * SUBLANES * LANES * 4
PEER_VMEM_LIMIT = PEER_TABLE_BYTES + (20 << 20)


def _peer_pack_kernel(lo_ref, hi_ref, o_ref):
    f32 = jnp.float32
    lo = pltpu.bitcast(_bf16(lo_ref[...]).astype(f32), jnp.uint32)
    hi = pltpu.bitcast(_bf16(hi_ref[...]).astype(f32), jnp.uint32)
    o_ref[...] = (hi & jnp.uint32(0xFFFF0000)) | (lo >> 16)


def peer_pack_table(table):
    rows = PAIRED_EXPERTS * SUBLANES
    t2 = table.reshape(2 * rows, LANES)
    nblk = rows // PACK_ROW_BLOCK
    return pl.pallas_call(
        _peer_pack_kernel,
        grid=(nblk,),
        in_specs=[pl.BlockSpec((PACK_ROW_BLOCK, LANES), lambda i: (i, 0)),
                  pl.BlockSpec((PACK_ROW_BLOCK, LANES), lambda i: (i + nblk, 0))],
        out_specs=pl.BlockSpec((PACK_ROW_BLOCK, LANES), lambda i: (i, 0)),
        out_shape=jax.ShapeDtypeStruct((rows, LANES), jnp.uint32),
        name="peer_pack",
    )(t2, t2)


def _slot_expander():
    j = lax.broadcasted_iota(jnp.int32, (PEER_SLOTS, PEER_K), 0)
    c = lax.broadcasted_iota(jnp.int32, (PEER_SLOTS, PEER_K), 1)
    return jnp.where((c >> TILE_ROWS_LOG2) == j, 1.0, 0.0)


def _to_token_major(a):
    return jnp.concatenate([a[:, g * LANES:(g + 1) * LANES].T for g in range(a.shape[1] // LANES)], axis=0)


def _half_selected(par_ref, expand):
    par_rep = jnp.dot(_bf16(_to_token_major(par_ref[...])), expand, preferred_element_type=jnp.float32)
    kbit = lax.broadcasted_iota(jnp.int32, par_rep.shape, 1) & 1
    return par_rep == kbit.astype(jnp.float32)


def _chunk_diagonal():
    s_out = lax.broadcasted_iota(jnp.int32, (SUBLANES, PEER_K), 0)
    c = lax.broadcasted_iota(jnp.int32, (SUBLANES, PEER_K), 1)
    return ((c & (TILE_ROWS - 1)) >> 1) == s_out


def _gather_tiles(row_ref, tab_ref, t):
    tiles = []
    for j in range(PEER_SLOTS):
        r = pl.multiple_of(row_ref[t, j], SUBLANES)
        tiles.append(pltpu.bitcast(tab_ref[pl.ds(r, SUBLANES), :], jnp.bfloat16))
    return jnp.concatenate(tiles, axis=0)


def _split_bf16_rows(a):
    hi = _bf16(a).astype(jnp.float32)
    return _bf16(jnp.concatenate([hi, a - hi], axis=0))


def _peer_u_kernel(row_ref, par_ref, x_ref, tab_ref, act_ref, keep_scr, r_scr):
    expand = _slot_expander()
    keep_scr[...] = jnp.where(_half_selected(par_ref, _bf16(expand)), 1.0, 0.0)
    diag = _chunk_diagonal()

    def tok_body(t, carry):
        m = _gather_tiles(row_ref, tab_ref, t)
        res = lax.dot_general(_split_bf16_rows(x_ref[t]), m, NT_DIMS, preferred_element_type=jnp.float32)
        r = jnp.where(diag, res[:SUBLANES] + res[SUBLANES:], 0.0)
        r_scr[pl.ds(t, 1), :] = jnp.sum(r, axis=0, keepdims=True) * keep_scr[pl.ds(t, 1), :]
        return carry

    lax.fori_loop(0, PEER_TOK_BLOCK, tok_body, 0, unroll=PEER_TOK_UNROLL)
    act_ref[...] = lax.dot_general(expand, r_scr[...], NT_DIMS, precision=HIGHEST,
                                   preferred_element_type=jnp.float32)


def _peer_v_kernel(row_ref, par_ref, act_ref, gate_ref, tab_ref, out_ref, whi_scr, wlo_scr):
    act = act_ref[...]
    w = _to_token_major(gate_ref[...] * (0.5 * act * (1.0 + lax.erf(act * (2.0 ** -0.5)))))
    expand = _bf16(_slot_expander())
    keep = _half_selected(par_ref, expand)
    w_hi = _bf16(w)
    w_lo = _bf16(w - w_hi.astype(jnp.float32))
    whi_scr[...] = jnp.where(keep, jnp.dot(w_hi, expand, preferred_element_type=jnp.float32), 0.0)
    wlo_scr[...] = jnp.where(keep, jnp.dot(w_lo, expand, preferred_element_type=jnp.float32), 0.0)
    diag = _chunk_diagonal()

    def tok_body(t, carry):
        m = _gather_tiles(row_ref, tab_ref, t)
        lhs = jnp.concatenate([jnp.where(diag, whi_scr[pl.ds(t, 1), :], 0.0),
                               jnp.where(diag, wlo_scr[pl.ds(t, 1), :], 0.0)], axis=0)
        res = jnp.dot(_bf16(lhs), m, preferred_element_type=jnp.float32)
        out_ref[t] = res[:SUBLANES] + res[SUBLANES:]
        return carry

    lax.fori_loop(0, PEER_TOK_BLOCK, tok_body, 0, unroll=PEER_TOK_UNROLL)


_PEER_PARAMS = pltpu.CompilerParams(dimension_semantics=("arbitrary",), vmem_limit_bytes=PEER_VMEM_LIMIT)


def peer_experts(xn, e_t, g_t, peer_u, peer_v):
    n = xn.shape[0]
    rows = ((e_t % PAIRED_EXPERTS) * SUBLANES).T
    par = (e_t // PAIRED_EXPERTS).astype(jnp.float32)
    smem_rows = pl.BlockSpec((PEER_TOK_BLOCK, PEER_SLOTS), lambda i: (i, 0), memory_space=pltpu.SMEM)
    slots = pl.BlockSpec((PEER_SLOTS, PEER_TOK_BLOCK), lambda i: (0, i))
    tiles = pl.BlockSpec((PEER_TOK_BLOCK, SUBLANES, LANES), lambda i: (i, 0, 0))
    table = _resident((PAIRED_EXPERTS * SUBLANES, LANES))
    wide = pltpu.VMEM((PEER_TOK_BLOCK, PEER_K), jnp.float32)
    act = pl.pallas_call(
        _peer_u_kernel,
        grid=(n // PEER_TOK_BLOCK,),
        in_specs=[smem_rows, slots, tiles, table],
        out_specs=slots,
        out_shape=jax.ShapeDtypeStruct((PEER_SLOTS, n), jnp.float32),
        scratch_shapes=[wide, wide],
        compiler_params=_PEER_PARAMS,
        name="peer_u",
    )(rows, par, xn.reshape(n, SUBLANES, LANES), peer_pack_table(peer_u))
    out = pl.pallas_call(
        _peer_v_kernel,
        grid=(n // PEER_TOK_BLOCK,),
        in_specs=[smem_rows, slots, slots, slots, table],
        out_specs=tiles,
        out_shape=jax.ShapeDtypeStruct((n, SUBLANES, LANES), jnp.float32),
        scratch_shapes=[wide, wide],
        compiler_params=_PEER_PARAMS,
        name="peer_v",
    )(rows, par, act, g_t, peer_pack_table(peer_v))
    return out.reshape(n, D_MODEL)


def _ple_kernel(x_ref, o_ref, p_ref, gple_ref, wpg_ref, wple_ref, gfin_ref, y_ref):
    x = x_ref[...] + o_ref[...]
    gate = jax.nn.sigmoid(_dot(_rms(x, gple_ref[...]), wpg_ref[...]))
    x = x + _dot(p_ref[...], wple_ref[...]) * gate
    y_ref[...] = _rms(x, gfin_ref[...])


def ple_and_final_norm(x, peer_out, p, g_ple, w_pg, w_ple, g_final):
    n = x.shape[0]
    row = lambda i: (i, 0)
    blk = pl.BlockSpec((ROW_BLOCK, D_MODEL), row)
    return pl.pallas_call(
        _ple_kernel,
        grid=(n // ROW_BLOCK,),
        in_specs=[blk, blk, pl.BlockSpec((ROW_BLOCK, D_PLE), row), _resident((1, D_MODEL)),
                  _resident((D_MODEL, D_MODEL)), _resident((D_PLE, D_MODEL)), _resident((1, D_MODEL))],
        out_specs=blk,
        out_shape=jax.ShapeDtypeStruct((n, D_MODEL), jnp.float32),
        compiler_params=pltpu.CompilerParams(dimension_semantics=("arbitrary",),
                                             vmem_limit_bytes=DENSE_VMEM_LIMIT),
        name="ple_final",
    )(x, peer_out, p, g_ple.reshape(1, D_MODEL), _bf16(w_pg), _bf16(w_ple), g_final.reshape(1, D_MODEL))


def kernel(x_prompt, x_sample, p_prompt, p_sample, state_ret, state_mlstm_C, state_mlstm_n,
           state_mlstm_m, state_conv, g_mix, w_in, g_ret_gn, w_mq, w_mk, conv_w, conv_b, b_i, b_f,
           g_ml_gn, w_skip, w_up_r, w_up_m, w_out, g_ffn, w_pq, peer_keys, peer_u, peer_v,
           g_ple, w_pg, w_ple, g_final):
    f32 = jnp.float32
    Bp = x_prompt.shape[0]
    pos_p = jnp.arange(x_prompt.shape[1], dtype=jnp.int32)
    pos_s = PAST_LEN + jnp.arange(x_sample.shape[1], dtype=jnp.int32)
    z_ret = jnp.zeros((Bp, H_RET, DH_RET, DH_RET), f32)
    z_c = jnp.zeros((Bp, H_ML, DH_ML, DH_ML), f32)
    z_n = jnp.zeros((Bp, H_ML, DH_ML), f32)
    z_m = jnp.zeros((Bp, H_ML), f32)
    z_buf = jnp.zeros((Bp, CONV_W - 1, W_ML), x_prompt.dtype)
    mixer_w = (g_ret_gn[0], w_mq[0], w_mk[0], conv_w[0], conv_b[0], b_i[0], b_f[0], g_ml_gn[0], w_skip[0])

    def stream(x, pos, s0, c0, n0, m0, cbuf):
        B, T, _ = x.shape
        x2 = x.reshape(B * T, D_MODEL)
        zr, zm, zif, zg = in_proj(x2, g_mix[0], w_in[0])
        seq = lambda z: z.reshape(B, T, z.shape[-1])
        y_r, y_m, *states = token_mixers(seq(zr), seq(zm), seq(zif), pos, s0, c0, n0, m0, cbuf, *mixer_w)
        x1 = merge_branches(x2, y_r.reshape(B * T, W_RET), y_m.reshape(B * T, W_ML), zg,
                            w_up_r[0], w_up_m[0], w_out[0])
        return x1, states

    hp, st_p = stream(x_prompt, pos_p, z_ret, z_c, z_n, z_m, z_buf)
    hs, st_s = stream(x_sample, pos_s, state_ret[0], state_mlstm_C[0], state_mlstm_n[0],
                      state_mlstm_m[0], state_conv[0])
    n_p = hp.shape[0]
    x_all = jnp.concatenate([hp, hs], axis=0)
    p_all = jnp.concatenate([p_prompt[0].reshape(n_p, D_PLE), p_sample[0].reshape(-1, D_PLE)], axis=0)
    xn, e_t, g_t = peer_route(x_all, g_ffn[0], w_pq[0], peer_keys[0])
    peer_out = peer_experts(xn, e_t, g_t, peer_u[0], peer_v[0])
    y_all = ple_and_final_norm(x_all, peer_out, p_all, g_ple[0], w_pg[0], w_ple[0], g_final)
    y_prompt = y_all[:n_p].reshape(x_prompt.shape)
    y_sample = y_all[n_p:].reshape(x_sample.shape)
    return (y_prompt, y_sample, *[a[None] for a in st_p], *[a[None] for a in st_s])
```

```python
import jax, jax.numpy as jnp
from jax import lax
import numpy as np
from jax.experimental import pallas as pl
from jax.experimental.pallas import tpu as pltpu

D_MODEL = 1024
PAST_LEN = 2048
CHUNK = 64
EPS = 1e-6
ROPE_BASE = 10000.0
H_RET = 4
W_RET = D_MODEL // 2
DH_RET = W_RET // H_RET
H_ML = 4
W_ML = D_MODEL // 2
DH_ML = W_ML // H_ML
CONV_W = 4
PEER_HEADS = 8
N_KEYS = 128
N_EXPERTS = N_KEYS * N_KEYS
PEER_TOPK = 16
PEER_DQ = 256
PEER_DQ_HALF = PEER_DQ // 2
D_PLE = 256

SUBLANES = 8
LANES = 128
NEG_INF = float("-inf")
HIGHEST = lax.Precision.HIGHEST
NT_DIMS = (((1,), (1,)), ((), ()))


def _bf16(a):
    return a.astype(jnp.bfloat16)


def _dot(a, b):
    return jnp.dot(_bf16(a), _bf16(b), preferred_element_type=jnp.float32)


def _dot_nt(a, b):
    return lax.dot_general(_bf16(a), _bf16(b), NT_DIMS, preferred_element_type=jnp.float32)


def _pad_rows(a, rows):
    return jnp.concatenate([a, jnp.zeros((rows - a.shape[0], a.shape[1]), a.dtype)], axis=0)


def _dot_tn(a, b):
    return _dot(_pad_rows(a, LANES).T, _pad_rows(b, LANES))


def _rms(x, g):
    return x * lax.rsqrt(jnp.mean(x * x, axis=-1, keepdims=True) + EPS) * g


def _resident(shape):
    return pl.BlockSpec(shape, lambda *_: (0,) * len(shape), pipeline_mode=pl.Buffered(1))


ROW_BLOCK = 256
DENSE_VMEM_LIMIT = 48 << 20
Z_RET = 4 * W_RET
Z_ML = 3 * W_ML
Z_GATE = 2 * D_MODEL
SCAN_BATCH = 4


def _in_proj_kernel(x_ref, g_ref, wr_ref, wm_ref, wif_ref, wg_ref, zr_ref, zm_ref, zif_ref, zg_ref):
    h = _bf16(_rms(x_ref[...], g_ref[...]))
    for w_ref, z_ref in ((wr_ref, zr_ref), (wm_ref, zm_ref), (wif_ref, zif_ref), (wg_ref, zg_ref)):
        z_ref[...] = jnp.dot(h, w_ref[...], preferred_element_type=jnp.float32)


def in_proj(x, g_mix, w_in):
    n = x.shape[0]
    w = _bf16(w_in)
    c_if = Z_RET + Z_ML
    w_if = jnp.pad(w[:, c_if:c_if + 2 * H_ML], ((0, 0), (0, LANES - 2 * H_ML)))
    widths = (Z_RET, Z_ML, LANES, Z_GATE)
    row = lambda i: (i, 0)
    return pl.pallas_call(
        _in_proj_kernel,
        grid=(n // ROW_BLOCK,),
        in_specs=[pl.BlockSpec((ROW_BLOCK, D_MODEL), row), _resident((1, D_MODEL))]
        + [_resident((D_MODEL, wd)) for wd in widths],
        out_specs=[pl.BlockSpec((ROW_BLOCK, wd), row) for wd in widths],
        out_shape=[jax.ShapeDtypeStruct((n, wd), jnp.float32) for wd in widths],
        compiler_params=pltpu.CompilerParams(dimension_semantics=("arbitrary",),
                                             vmem_limit_bytes=DENSE_VMEM_LIMIT),
        name="in_proj",
    )(x, g_mix.reshape(1, D_MODEL), w[:, :Z_RET], w[:, Z_RET:c_if], w_if, w[:, c_if + 2 * H_ML:])


def _head_norm(y):
    return y * lax.rsqrt(jnp.mean(y * y, axis=-1, keepdims=True) + EPS)


def _scan_kernel(zr_ref, zm_ref, zif_ref, cos_ref, sin_ref, s0_ref, c0_ref, n0_ref, m0_ref, cb0_ref,
                 gret_ref, wmq_ref, wmk_ref, convw_ref, convb_ref, bif_ref, gml_ref, wskip_ref,
                 yr_ref, ym_ref, sret_ref, cml_ref, nml_ref, mml_ref, ctail_ref,
                 s_scr, c_scr, n_scr, m_scr, xprev_scr):
    nb, L = zr_ref.shape[0], zr_ref.shape[1]
    chunk = pl.program_id(1)

    @pl.when(chunk == 0)
    def _():
        s_scr[...] = s0_ref[...]
        c_scr[...] = c0_ref[...]
        n_scr[...] = n0_ref[...]
        m_scr[...] = m0_ref[...]
        xprev_scr[...] = cb0_ref[...]

    f32 = jnp.float32
    row = lax.broadcasted_iota(jnp.int32, (L, L), 0)
    col = lax.broadcasted_iota(jnp.int32, (L, L), 1)
    causal = row >= col
    diff = (row - col).astype(f32)
    tril = jnp.where(causal, 1.0, 0.0).astype(f32)
    idx = lax.broadcasted_iota(jnp.int32, (L, 1), 0).astype(f32)
    cos, sin = cos_ref[...], sin_ref[...]
    half = DH_RET // 2

    for b in range(nb):
        zr = zr_ref[b]
        parts = []
        for h in range(H_RET):
            lg = float(np.log1p(-(2.0 ** (-5 - h))))
            sl = slice(h * DH_RET, (h + 1) * DH_RET)
            q = zr[:, sl]
            k = zr[:, W_RET + h * DH_RET:W_RET + (h + 1) * DH_RET]
            v = zr[:, 2 * W_RET + h * DH_RET:2 * W_RET + (h + 1) * DH_RET]
            gt = zr[:, 3 * W_RET + h * DH_RET:3 * W_RET + (h + 1) * DH_RET]
            q = q * cos + pltpu.roll(q, half, 1) * sin
            k = (k * cos + pltpu.roll(k, half, 1) * sin) * (DH_RET ** -0.5)
            dmask = jnp.where(causal, jnp.exp(lg * jnp.maximum(diff, 0.0)), 0.0)
            s_prev = s_scr[b, h]
            o = _dot(_dot_nt(q, k) * dmask, v)
            o = o + _dot(q * jnp.exp(lg * (idx + 1.0)), s_prev)
            ds = _dot_tn(k * jnp.exp(lg * (L - 1 - idx)), v)
            s_scr[b, h] = float(np.exp(lg * L)) * s_prev + ds
            parts.append(jax.nn.silu(gt) * (_head_norm(o) * gret_ref[:, sl]))
        yr_ref[b] = _bf16(jnp.concatenate(parts, axis=1))

        zm = zm_ref[b]
        xm = zm[:, :W_ML]
        xp = jnp.concatenate([xprev_scr[b], xm], axis=0)
        first = SUBLANES - (CONV_W - 1)
        xc = convb_ref[...] + xp[first:first + L] * convw_ref[0:1, :]
        for j in range(1, CONV_W):
            xc = xc + xp[first + j:first + j + L] * convw_ref[j:j + 1, :]
        xprev_scr[b] = xm[L - SUBLANES:, :]
        cact = jax.nn.silu(xc)
        gates = zif_ref[b] + bif_ref[...]
        fcum = jnp.dot(tril, jax.nn.log_sigmoid(gates), precision=HIGHEST, preferred_element_type=f32)
        gates_t = _pad_rows(gates, LANES).T
        fcum_t = _pad_rows(fcum, LANES).T
        parts = []
        for h in range(H_ML):
            sl = slice(h * DH_ML, (h + 1) * DH_ML)
            ch = cact[:, sl]
            q = _dot(ch, wmq_ref[h])
            k = _dot(ch, wmk_ref[h]) * (DH_ML ** -0.5)
            v = zm[:, W_ML + h * DH_ML:W_ML + (h + 1) * DH_ML]
            o_gate = zm[:, 2 * W_ML + h * DH_ML:2 * W_ML + (h + 1) * DH_ML]
            ig_row, f_row = gates_t[h:h + 1, :L], fcum_t[H_ML + h:H_ML + h + 1, :L]
            ig_col, f_col = gates[:, h:h + 1], fcum[:, H_ML + h:H_ML + h + 1]
            m_prev = m_scr[b, h:h + 1, 0:1]
            c_prev = c_scr[b, h]
            n_prev = n_scr[b, h:h + 1, :]
            log_d = jnp.where(causal, ig_row + f_col - f_row, NEG_INF)
            inter = m_prev + f_col
            m_t = jnp.maximum(inter, jnp.max(log_d, axis=1, keepdims=True))
            a = jnp.exp(inter - m_t)
            s = _dot_nt(q, k) * jnp.exp(log_d - m_t)
            num = _dot(s, v) + a * _dot(q, c_prev)
            den = jnp.sum(s, axis=1, keepdims=True) + a * jnp.sum(q * n_prev, axis=1, keepdims=True)
            hc = num / jnp.maximum(jnp.abs(den), jnp.exp(-m_t))
            m_new = m_t[L - 1:L, :]
            f_last = f_col[L - 1:L, :]
            a_end = jnp.exp(m_prev + f_last - m_new)
            kw = k * jnp.exp(ig_col + f_last - f_col - m_new)
            c_scr[b, h] = a_end * c_prev + _dot_tn(kw, v)
            n_scr[b, h:h + 1, :] = a_end * n_prev + jnp.sum(kw, axis=0, keepdims=True)
            m_scr[b, h:h + 1, :] = jnp.broadcast_to(m_new, (1, LANES))
            parts.append(jax.nn.sigmoid(o_gate) * (_head_norm(hc) * gml_ref[:, sl] + wskip_ref[:, sl] * ch))
        ym_ref[b] = _bf16(jnp.concatenate(parts, axis=1))

    @pl.when(chunk == pl.num_programs(1) - 1)
    def _():
        sret_ref[...] = s_scr[...]
        cml_ref[...] = c_scr[...]
        nml_ref[...] = n_scr[...]
        mml_ref[...] = m_scr[...]
        ctail_ref[...] = xprev_scr[...]


def _rope_tables(pos):
    inv = ROPE_BASE ** (-jnp.arange(DH_RET // 2, dtype=jnp.float32) / (DH_RET // 2))
    ang = pos.astype(jnp.float32)[:, None] * inv[None, :]
    cos, sin = jnp.cos(ang), jnp.sin(ang)
    return jnp.concatenate([cos, cos], axis=1), jnp.concatenate([-sin, sin], axis=1)


def token_mixers(zr, zm, zif, pos, s0, c0, n0, m0, conv_buf,
                 g_ret_gn, w_mq, w_mk, conv_w, conv_b, b_i, b_f, g_ml_gn, w_skip):
    B, T = zr.shape[0], zr.shape[1]
    L = min(CHUNK, T)
    nb = SCAN_BATCH
    cos2, sin2 = _rope_tables(pos)
    f32 = jnp.float32
    m0b = jnp.broadcast_to(m0.astype(f32)[:, :, None], (B, H_ML, LANES))
    cb0 = jnp.pad(conv_buf.astype(f32), ((0, 0), (SUBLANES - (CONV_W - 1), 0), (0, 0)))
    bif = jnp.pad(jnp.concatenate([b_i, b_f]).astype(f32), (0, LANES - 2 * H_ML)).reshape(1, LANES)
    seq = lambda wd: pl.BlockSpec((nb, L, wd), lambda i, c: (i, c, 0))
    tab = pl.BlockSpec((L, DH_RET), lambda i, c: (c, 0))
    st4 = pl.BlockSpec((nb, H_RET, DH_RET, DH_RET), lambda i, c: (i, 0, 0, 0))
    st3 = pl.BlockSpec((nb, H_ML, DH_ML), lambda i, c: (i, 0, 0))
    cb = pl.BlockSpec((nb, SUBLANES, W_ML), lambda i, c: (i, 0, 0))
    row512 = _resident((1, W_ML))
    outs = pl.pallas_call(
        _scan_kernel,
        grid=(B // nb, T // L),
        in_specs=[seq(Z_RET), seq(Z_ML), seq(LANES), tab, tab, st4, st4, st3, st3, cb,
                  row512, _resident((H_ML, DH_ML, DH_ML)), _resident((H_ML, DH_ML, DH_ML)),
                  _resident((CONV_W, W_ML)), row512, _resident((1, LANES)), row512, row512],
        out_specs=[seq(W_RET), seq(W_ML), st4, st4, st3, st3, cb],
        out_shape=[jax.ShapeDtypeStruct((B, T, W_RET), jnp.bfloat16),
                   jax.ShapeDtypeStruct((B, T, W_ML), jnp.bfloat16),
                   jax.ShapeDtypeStruct((B, H_RET, DH_RET, DH_RET), f32),
                   jax.ShapeDtypeStruct((B, H_ML, DH_ML, DH_ML), f32),
                   jax.ShapeDtypeStruct((B, H_ML, DH_ML), f32),
                   jax.ShapeDtypeStruct((B, H_ML, LANES), f32),
                   jax.ShapeDtypeStruct((B, SUBLANES, W_ML), f32)],
        scratch_shapes=[pltpu.VMEM((nb, H_RET, DH_RET, DH_RET), f32),
                        pltpu.VMEM((nb, H_ML, DH_ML, DH_ML), f32),
                        pltpu.VMEM((nb, H_ML, DH_ML), f32),
                        pltpu.VMEM((nb, H_ML, LANES), f32),
                        pltpu.VMEM((nb, SUBLANES, W_ML), f32)],
        compiler_params=pltpu.CompilerParams(dimension_semantics=("arbitrary", "arbitrary"),
                                             vmem_limit_bytes=DENSE_VMEM_LIMIT),
        name="token_mixers",
    )(zr, zm, zif, cos2, sin2, s0.astype(f32), c0.astype(f32), n0.astype(f32), m0b, cb0,
      g_ret_gn.reshape(1, W_RET), _bf16(w_mq), _bf16(w_mk), conv_w, conv_b.reshape(1, W_ML), bif,
      g_ml_gn.reshape(1, W_ML), w_skip.reshape(1, W_ML))
    y_r, y_m, s_new, c_new, n_new, m_new, ctail = outs
    return y_r, y_m, s_new, c_new, n_new, m_new[:, :, 0], ctail[:, SUBLANES - (CONV_W - 1):, :]


def _merge_kernel(x_ref, yr_ref, ym_ref, zg_ref, wur_ref, wum_ref, wo_ref, o_ref):
    zg = zg_ref[...]
    up_r = jnp.dot(yr_ref[...], wur_ref[...], preferred_element_type=jnp.float32)
    up_m = jnp.dot(ym_ref[...], wum_ref[...], preferred_element_type=jnp.float32)
    merged = jax.nn.sigmoid(zg[:, :D_MODEL]) * up_r + jax.nn.sigmoid(zg[:, D_MODEL:]) * up_m
    o_ref[...] = x_ref[...] + jnp.dot(_bf16(merged), wo_ref[...], preferred_element_type=jnp.float32)


def merge_branches(x, y_r, y_m, zg, w_up_r, w_up_m, w_out):
    n = x.shape[0]
    row = lambda i: (i, 0)
    return pl.pallas_call(
        _merge_kernel,
        grid=(n // ROW_BLOCK,),
        in_specs=[pl.BlockSpec((ROW_BLOCK, D_MODEL), row), pl.BlockSpec((ROW_BLOCK, W_RET), row),
                  pl.BlockSpec((ROW_BLOCK, W_ML), row), pl.BlockSpec((ROW_BLOCK, Z_GATE), row),
                  _resident((W_RET, D_MODEL)), _resident((W_ML, D_MODEL)), _resident((D_MODEL, D_MODEL))],
        out_specs=pl.BlockSpec((ROW_BLOCK, D_MODEL), row),
        out_shape=jax.ShapeDtypeStruct((n, D_MODEL), jnp.float32),
        compiler_params=pltpu.CompilerParams(dimension_semantics=("arbitrary",),
                                             vmem_limit_bytes=DENSE_VMEM_LIMIT),
        name="merge_branches",
    )(x, y_r, y_m, zg, _bf16(w_up_r), _bf16(w_up_m), _bf16(w_out))


PEER_SLOTS = PEER_HEADS * PEER_TOPK
ROUTE_TOK_BLOCK = 256
_CAND_GROUPS = ((0, 0), (0, 8), (1, 0)) + tuple((a, 0) for a in range(2, 8))


def _take_top16_rows(s, code):
    big = jnp.float32(1e9)
    vals, codes = [], []
    for _ in range(PEER_TOPK):
        m = jnp.max(s, axis=0, keepdims=True)
        c = jnp.min(jnp.where(s == m, code, big), axis=0, keepdims=True)
        s = jnp.where(code == c, NEG_INF, s)
        vals.append(m)
        codes.append(c)
    return jnp.concatenate(vals, axis=0), jnp.concatenate(codes, axis=0)


def _lookup_rows(table, sel):
    out = jnp.zeros_like(sel)
    for a in range(PEER_TOPK):
        out = jnp.where(sel == jnp.float32(a), table[a:a + 1, :], out)
    return out


def _route_one_head(s1, s2):
    key_id = lax.broadcasted_iota(jnp.int32, (N_KEYS, LANES), 0).astype(jnp.float32)
    v1, i1 = _take_top16_rows(s1, key_id)
    v2, i2 = _take_top16_rows(s2, key_id)
    sub = lax.broadcasted_iota(jnp.int32, (SUBLANES, LANES), 0)
    subf = sub.astype(jnp.float32)
    cand, code = [], []
    for a, b0 in _CAND_GROUPS:
        c = v1[a:a + 1, :] + v2[b0:b0 + SUBLANES, :]
        n_valid = PEER_TOPK // (a + 1) - b0
        cand.append(c if n_valid >= SUBLANES else jnp.where(sub < n_valid, c, NEG_INF))
        code.append(subf + jnp.float32(a * PEER_TOPK + b0))
    cand.append(v1[SUBLANES:, :] + v2[0:1, :])
    code.append((subf + jnp.float32(SUBLANES)) * jnp.float32(PEER_TOPK))
    sc, cc = _take_top16_rows(jnp.concatenate(cand, axis=0), jnp.concatenate(code, axis=0))
    a_sel = jnp.floor(cc * jnp.float32(1.0 / PEER_TOPK))
    b_sel = cc - a_sel * jnp.float32(PEER_TOPK)
    e = _lookup_rows(i1, a_sel) * jnp.float32(N_KEYS) + _lookup_rows(i2, b_sel)
    ex = jnp.exp(sc - sc[0:1, :])
    gate = ex / jnp.sum(ex, axis=0, keepdims=True)
    return e.astype(jnp.int32), gate


def _peer_route_kernel(x_ref, g_ref, wpq_ref, keys_ref, xn_ref, e_ref, gate_ref, q_scr):
    xn = _rms(x_ref[...], g_ref[...])
    xn_ref[...] = xn
    q = jnp.dot(_bf16(xn), wpq_ref[...], preferred_element_type=jnp.float32)
    for c in range(2 * PEER_HEADS):
        q_scr[c] = _bf16(q[:, c * PEER_DQ_HALF:(c + 1) * PEER_DQ_HALF])

    def head_body(n, carry):
        rows = pl.ds(pl.multiple_of(n * PEER_TOPK, PEER_TOPK), PEER_TOPK)
        for lg in range(ROUTE_TOK_BLOCK // LANES):
            toks = slice(lg * LANES, (lg + 1) * LANES)
            s = [lax.dot_general(keys_ref[2 * n + h], q_scr[2 * n + h, toks, :], NT_DIMS,
                                 preferred_element_type=jnp.float32) for h in range(2)]
            e, gate = _route_one_head(s[0], s[1])
            e_ref[rows, toks] = e
            gate_ref[rows, toks] = gate
        return carry

    lax.fori_loop(0, PEER_HEADS, head_body, 0)


def peer_route(x, g_ffn, w_pq, peer_keys):
    n = x.shape[0]
    tok = lambda i: (i, 0)
    slots = pl.BlockSpec((PEER_SLOTS, ROUTE_TOK_BLOCK), lambda i: (0, i))
    return pl.pallas_call(
        _peer_route_kernel,
        grid=(n // ROUTE_TOK_BLOCK,),
        in_specs=[pl.BlockSpec((ROUTE_TOK_BLOCK, D_MODEL), tok),
                  _resident((1, D_MODEL)),
                  _resident((D_MODEL, PEER_HEADS * PEER_DQ)),
                  _resident((2 * PEER_HEADS, N_KEYS, PEER_DQ_HALF))],
        out_specs=[pl.BlockSpec((ROUTE_TOK_BLOCK, D_MODEL), tok), slots, slots],
        out_shape=[jax.ShapeDtypeStruct((n, D_MODEL), jnp.float32),
                   jax.ShapeDtypeStruct((PEER_SLOTS, n), jnp.int32),
                   jax.ShapeDtypeStruct((PEER_SLOTS, n), jnp.float32)],
        scratch_shapes=[pltpu.VMEM((2 * PEER_HEADS, ROUTE_TOK_BLOCK, PEER_DQ_HALF), jnp.bfloat16)],
        compiler_params=pltpu.CompilerParams(dimension_semantics=("arbitrary",),
                                             vmem_limit_bytes=DENSE_VMEM_LIMIT),
        name="peer_route",
    )(x, g_ffn.reshape(1, D_MODEL), _bf16(w_pq),
      _bf16(peer_keys.reshape(2 * PEER_HEADS, N_KEYS, PEER_DQ_HALF)))


PAIRED_EXPERTS = N_EXPERTS // 2
TILE_ROWS = 2 * SUBLANES
TILE_ROWS_LOG2 = TILE_ROWS.bit_length() - 1
PEER_K = PEER_SLOTS * TILE_ROWS
PEER_TOK_BLOCK = 256
PEER_TOK_UNROLL = 8
PACK_ROW_BLOCK = 2048
PEER_TABLE_BYTES = PAIRED_EXPERTS * SUBLANES * LANES * 4
PEER_VMEM_LIMIT = PEER_TABLE_BYTES + (20 << 20)


def _peer_pack_kernel(lo_ref, hi_ref, o_ref):
    f32 = jnp.float32
    lo = pltpu.bitcast(_bf16(lo_ref[...]).astype(f32), jnp.uint32)
    hi = pltpu.bitcast(_bf16(hi_ref[...]).astype(f32), jnp.uint32)
    o_ref[...] = (hi & jnp.uint32(0xFFFF0000)) | (lo >> 16)


def peer_pack_table(table):
    rows = PAIRED_EXPERTS * SUBLANES
    t2 = table.reshape(2 * rows, LANES)
    nblk = rows // PACK_ROW_BLOCK
    return pl.pallas_call(
        _peer_pack_kernel,
        grid=(nblk,),
        in_specs=[pl.BlockSpec((PACK_ROW_BLOCK, LANES), lambda i: (i, 0)),
                  pl.BlockSpec((PACK_ROW_BLOCK, LANES), lambda i: (i + nblk, 0))],
        out_specs=pl.BlockSpec((PACK_ROW_BLOCK, LANES), lambda i: (i, 0)),
        out_shape=jax.ShapeDtypeStruct((rows, LANES), jnp.uint32),
        name="peer_pack",
    )(t2, t2)


def _slot_expander():
    j = lax.broadcasted_iota(jnp.int32, (PEER_SLOTS, PEER_K), 0)
    c = lax.broadcasted_iota(jnp.int32, (PEER_SLOTS, PEER_K), 1)
    return jnp.where((c >> TILE_ROWS_LOG2) == j, 1.0, 0.0)


def _to_token_major(a):
    return jnp.concatenate([a[:, g * LANES:(g + 1) * LANES].T for g in range(a.shape[1] // LANES)], axis=0)


def _half_selected(par_ref, expand):
    par_rep = jnp.dot(_bf16(_to_token_major(par_ref[...])), expand, preferred_element_type=jnp.float32)
    kbit = lax.broadcasted_iota(jnp.int32, par_rep.shape, 1) & 1
    return par_rep == kbit.astype(jnp.float32)


def _chunk_diagonal():
    s_out = lax.broadcasted_iota(jnp.int32, (SUBLANES, PEER_K), 0)
    c = lax.broadcasted_iota(jnp.int32, (SUBLANES, PEER_K), 1)
    return ((c & (TILE_ROWS - 1)) >> 1) == s_out


def _gather_tiles(row_ref, tab_ref, t):
    tiles = []
    for j in range(PEER_SLOTS):
        r = pl.multiple_of(row_ref[t, j], SUBLANES)
        tiles.append(pltpu.bitcast(tab_ref[pl.ds(r, SUBLANES), :], jnp.bfloat16))
    return jnp.concatenate(tiles, axis=0)


def _split_bf16_rows(a):
    hi = _bf16(a).astype(jnp.float32)
    return _bf16(jnp.concatenate([hi, a - hi], axis=0))


def _peer_u_kernel(row_ref, par_ref, x_ref, tab_ref, act_ref, keep_scr, r_scr):
    expand = _slot_expander()
    keep_scr[...] = jnp.where(_half_selected(par_ref, _bf16(expand)), 1.0, 0.0)
    diag = _chunk_diagonal()

    def tok_body(t, carry):
        m = _gather_tiles(row_ref, tab_ref, t)
        res = lax.dot_general(_split_bf16_rows(x_ref[t]), m, NT_DIMS, preferred_element_type=jnp.float32)
        r = jnp.where(diag, res[:SUBLANES] + res[SUBLANES:], 0.0)
        r_scr[pl.ds(t, 1), :] = jnp.sum(r, axis=0, keepdims=True) * keep_scr[pl.ds(t, 1), :]
        return carry

    lax.fori_loop(0, PEER_TOK_BLOCK, tok_body, 0, unroll=PEER_TOK_UNROLL)
    act_ref[...] = lax.dot_general(expand, r_scr[...], NT_DIMS, precision=HIGHEST,
                                   preferred_element_type=jnp.float32)


def _peer_v_kernel(row_ref, par_ref, act_ref, gate_ref, tab_ref, out_ref, whi_scr, wlo_scr):
    act = act_ref[...]
    w = _to_token_major(gate_ref[...] * (0.5 * act * (1.0 + lax.erf(act * (2.0 ** -0.5)))))
    expand = _bf16(_slot_expander())
    keep = _half_selected(par_ref, expand)
    w_hi = _bf16(w)
    w_lo = _bf16(w - w_hi.astype(jnp.float32))
    whi_scr[...] = jnp.where(keep, jnp.dot(w_hi, expand, preferred_element_type=jnp.float32), 0.0)
    wlo_scr[...] = jnp.where(keep, jnp.dot(w_lo, expand, preferred_element_type=jnp.float32), 0.0)
    diag = _chunk_diagonal()

    def tok_body(t, carry):
        m = _gather_tiles(row_ref, tab_ref, t)
        lhs = jnp.concatenate([jnp.where(diag, whi_scr[pl.ds(t, 1), :], 0.0),
                               jnp.where(diag, wlo_scr[pl.ds(t, 1), :], 0.0)], axis=0)
        res = jnp.dot(_bf16(lhs), m, preferred_element_type=jnp.float32)
        out_ref[t] = res[:SUBLANES] + res[SUBLANES:]
        return carry

    lax.fori_loop(0, PEER_TOK_BLOCK, tok_body, 0, unroll=PEER_TOK_UNROLL)


_PEER_PARAMS = pltpu.CompilerParams(dimension_semantics=("arbitrary",), vmem_limit_bytes=PEER_VMEM_LIMIT)


def peer_experts(xn, e_t, g_t, peer_u, peer_v):
    n = xn.shape[0]
    rows = ((e_t % PAIRED_EXPERTS) * SUBLANES).T
    par = (e_t // PAIRED_EXPERTS).astype(jnp.float32)
    smem_rows = pl.BlockSpec((PEER_TOK_BLOCK, PEER_SLOTS), lambda i: (i, 0), memory_space=pltpu.SMEM)
    slots = pl.BlockSpec((PEER_SLOTS, PEER_TOK_BLOCK), lambda i: (0, i))
    tiles = pl.BlockSpec((PEER_TOK_BLOCK, SUBLANES, LANES), lambda i: (i, 0, 0))
    table = _resident((PAIRED_EXPERTS * SUBLANES, LANES))
    wide = pltpu.VMEM((PEER_TOK_BLOCK, PEER_K), jnp.float32)
    act = pl.pallas_call(
        _peer_u_kernel,
        grid=(n // PEER_TOK_BLOCK,),
        in_specs=[smem_rows, slots, tiles, table],
        out_specs=slots,
        out_shape=jax.ShapeDtypeStruct((PEER_SLOTS, n), jnp.float32),
        scratch_shapes=[wide, wide],
        compiler_params=_PEER_PARAMS,
        name="peer_u",
    )(rows, par, xn.reshape(n, SUBLANES, LANES), peer_pack_table(peer_u))
    out = pl.pallas_call(
        _peer_v_kernel,
        grid=(n // PEER_TOK_BLOCK,),
        in_specs=[smem_rows, slots, slots, slots, table],
        out_specs=tiles,
        out_shape=jax.ShapeDtypeStruct((n, SUBLANES, LANES), jnp.float32),
        scratch_shapes=[wide, wide],
        compiler_params=_PEER_PARAMS,
        name="peer_v",
    )(rows, par, act, g_t, peer_pack_table(peer_v))
    return out.reshape(n, D_MODEL)


def _ple_kernel(x_ref, o_ref, p_ref, gple_ref, wpg_ref, wple_ref, gfin_ref, y_ref):
    x = x_ref[...] + o_ref[...]
    gate = jax.nn.sigmoid(_dot(_rms(x, gple_ref[...]), wpg_ref[...]))
    x = x + _dot(p_ref[...], wple_ref[...]) * gate
    y_ref[...] = _rms(x, gfin_ref[...])


def ple_and_final_norm(x, peer_out, p, g_ple, w_pg, w_ple, g_final):
    n = x.shape[0]
    row = lambda i: (i, 0)
    blk = pl.BlockSpec((ROW_BLOCK, D_MODEL), row)
    return pl.pallas_call(
        _ple_kernel,
        grid=(n // ROW_BLOCK,),
        in_specs=[blk, blk, pl.BlockSpec((ROW_BLOCK, D_PLE), row), _resident((1, D_MODEL)),
                  _resident((D_MODEL, D_MODEL)), _resident((D_PLE, D_MODEL)), _resident((1, D_MODEL))],
        out_specs=blk,
        out_shape=jax.ShapeDtypeStruct((n, D_MODEL), jnp.float32),
        compiler_params=pltpu.CompilerParams(dimension_semantics=("arbitrary",),
                                             vmem_limit_bytes=DENSE_VMEM_LIMIT),
        name="ple_final",
    )(x, peer_out, p, g_ple.reshape(1, D_MODEL), _bf16(w_pg), _bf16(w_ple), g_final.reshape(1, D_MODEL))


def kernel(x_prompt, x_sample, p_prompt, p_sample, state_ret, state_mlstm_C, state_mlstm_n,
           state_mlstm_m, state_conv, g_mix, w_in, g_ret_gn, w_mq, w_mk, conv_w, conv_b, b_i, b_f,
           g_ml_gn, w_skip, w_up_r, w_up_m, w_out, g_ffn, w_pq, peer_keys, peer_u, peer_v,
           g_ple, w_pg, w_ple, g_final):
    f32 = jnp.float32
    Bp = x_prompt.shape[0]
    pos_p = jnp.arange(x_prompt.shape[1], dtype=jnp.int32)
    pos_s = PAST_LEN + jnp.arange(x_sample.shape[1], dtype=jnp.int32)
    z_ret = jnp.zeros((Bp, H_RET, DH_RET, DH_RET), f32)
    z_c = jnp.zeros((Bp, H_ML, DH_ML, DH_ML), f32)
    z_n = jnp.zeros((Bp, H_ML, DH_ML), f32)
    z_m = jnp.zeros((Bp, H_ML), f32)
    z_buf = jnp.zeros((Bp, CONV_W - 1, W_ML), x_prompt.dtype)
    mixer_w = (g_ret_gn[0], w_mq[0], w_mk[0], conv_w[0], conv_b[0], b_i[0], b_f[0], g_ml_gn[0], w_skip[0])

    def stream(x, pos, s0, c0, n0, m0, cbuf):
        B, T, _ = x.shape
        x2 = x.reshape(B * T, D_MODEL)
        zr, zm, zif, zg = in_proj(x2, g_mix[0], w_in[0])
        seq = lambda z: z.reshape(B, T, z.shape[-1])
        y_r, y_m, *states = token_mixers(seq(zr), seq(zm), seq(zif), pos, s0, c0, n0, m0, cbuf, *mixer_w)
        x1 = merge_branches(x2, y_r.reshape(B * T, W_RET), y_m.reshape(B * T, W_ML), zg,
                            w_up_r[0], w_up_m[0], w_out[0])
        return x1, states

    hp, st_p = stream(x_prompt, pos_p, z_ret, z_c, z_n, z_m, z_buf)
    hs, st_s = stream(x_sample, pos_s, state_ret[0], state_mlstm_C[0], state_mlstm_n[0],
                      state_mlstm_m[0], state_conv[0])
    n_p = hp.shape[0]
    x_all = jnp.concatenate([hp, hs], axis=0)
    p_all = jnp.concatenate([p_prompt[0].reshape(n_p, D_PLE), p_sample[0].reshape(-1, D_PLE)], axis=0)
    xn, e_t, g_t = peer_route(x_all, g_ffn[0], w_pq[0], peer_keys[0])
    peer_out = peer_experts(xn, e_t, g_t, peer_u[0], peer_v[0])
    y_all = ple_and_final_norm(x_all, peer_out, p_all, g_ple[0], w_pg[0], w_ple[0], g_final)
    y_prompt = y_all[:n_p].reshape(x_prompt.shape)
    y_sample = y_all[n_p:].reshape(x_sample.shape)
    return (y_prompt, y_sample, *[a[None] for a in st_p], *[a[None] for a in st_s])
```

```python
import jax, jax.numpy as jnp
from jax import lax
import numpy as np
from jax.experimental import pallas as pl
from jax.experimental.pallas import tpu as pltpu

D_MODEL = 1024
PAST_LEN = 2048
CHUNK = 64
EPS = 1e-6
ROPE_BASE = 10000.0
H_RET = 4
W_RET = D_MODEL // 2
DH_RET = W_RET // H_RET
H_ML = 4
W_ML = D_MODEL // 2
DH_ML = W_ML // H_ML
CONV_W = 4
PEER_HEADS = 8
N_KEYS = 128
N_EXPERTS = N_KEYS * N_KEYS
PEER_TOPK = 16
PEER_DQ = 256
PEER_DQ_HALF = PEER_DQ // 2
D_PLE = 256

SUBLANES = 8
LANES = 128
NEG_INF = float("-inf")
HIGHEST = lax.Precision.HIGHEST
NT_DIMS = (((1,), (1,)), ((), ()))


def _bf16(a):
    return a.astype(jnp.bfloat16)


def _dot(a, b):
    return jnp.dot(_bf16(a), _bf16(b), preferred_element_type=jnp.float32)


def _dot_nt(a, b):
    return lax.dot_general(_bf16(a), _bf16(b), NT_DIMS, preferred_element_type=jnp.float32)


def _pad_rows(a, rows):
    return jnp.concatenate([a, jnp.zeros((rows - a.shape[0], a.shape[1]), a.dtype)], axis=0)


def _dot_tn(a, b):
    return _dot(_pad_rows(a, LANES).T, _pad_rows(b, LANES))


def _rms(x, g):
    return x * lax.rsqrt(jnp.mean(x * x, axis=-1, keepdims=True) + EPS) * g


def _resident(shape):
    return pl.BlockSpec(shape, lambda *_: (0,) * len(shape), pipeline_mode=pl.Buffered(1))


ROW_BLOCK = 256
DENSE_VMEM_LIMIT = 48 << 20
Z_RET = 4 * W_RET
Z_ML = 3 * W_ML
Z_GATE = 2 * D_MODEL
SCAN_BATCH = 4


def _in_proj_kernel(x_ref, g_ref, wr_ref, wm_ref, wif_ref, wg_ref, zr_ref, zm_ref, zif_ref, zg_ref):
    h = _bf16(_rms(x_ref[...], g_ref[...]))
    for w_ref, z_ref in ((wr_ref, zr_ref), (wm_ref, zm_ref), (wif_ref, zif_ref), (wg_ref, zg_ref)):
        z_ref[...] = jnp.dot(h, w_ref[...], preferred_element_type=jnp.float32)


def in_proj(x, g_mix, w_in):
    n = x.shape[0]
    w = _bf16(w_in)
    c_if = Z_RET + Z_ML
    w_if = jnp.pad(w[:, c_if:c_if + 2 * H_ML], ((0, 0), (0, LANES - 2 * H_ML)))
    widths = (Z_RET, Z_ML, LANES, Z_GATE)
    row = lambda i: (i, 0)
    return pl.pallas_call(
        _in_proj_kernel,
        grid=(n // ROW_BLOCK,),
        in_specs=[pl.BlockSpec((ROW_BLOCK, D_MODEL), row), _resident((1, D_MODEL))]
        + [_resident((D_MODEL, wd)) for wd in widths],
        out_specs=[pl.BlockSpec((ROW_BLOCK, wd), row) for wd in widths],
        out_shape=[jax.ShapeDtypeStruct((n, wd), jnp.float32) for wd in widths],
        compiler_params=pltpu.CompilerParams(dimension_semantics=("arbitrary",),
                                             vmem_limit_bytes=DENSE_VMEM_LIMIT),
        name="in_proj",
    )(x, g_mix.reshape(1, D_MODEL), w[:, :Z_RET], w[:, Z_RET:c_if], w_if, w[:, c_if + 2 * H_ML:])


def _head_norm(y):
    return y * lax.rsqrt(jnp.mean(y * y, axis=-1, keepdims=True) + EPS)


def _scan_kernel(zr_ref, zm_ref, zif_ref, cos_ref, sin_ref, s0_ref, c0_ref, n0_ref, m0_ref, cb0_ref,
                 gret_ref, wmq_ref, wmk_ref, convw_ref, convb_ref, bif_ref, gml_ref, wskip_ref,
                 yr_ref, ym_ref, sret_ref, cml_ref, nml_ref, mml_ref, ctail_ref,
                 s_scr, c_scr, n_scr, m_scr, xprev_scr):
    nb, L = zr_ref.shape[0], zr_ref.shape[1]
    chunk = pl.program_id(1)

    @pl.when(chunk == 0)
    def _():
        s_scr[...] = s0_ref[...]
        c_scr[...] = c0_ref[...]
        n_scr[...] = n0_ref[...]
        m_scr[...] = m0_ref[...]
        xprev_scr[...] = cb0_ref[...]

    f32 = jnp.float32
    row = lax.broadcasted_iota(jnp.int32, (L, L), 0)
    col = lax.broadcasted_iota(jnp.int32, (L, L), 1)
    causal = row >= col
    diff = (row - col).astype(f32)
    tril = jnp.where(causal, 1.0, 0.0).astype(f32)
    idx = lax.broadcasted_iota(jnp.int32, (L, 1), 0).astype(f32)
    cos, sin = cos_ref[...], sin_ref[...]
    half = DH_RET // 2

    for b in range(nb):
        zr = zr_ref[b]
        parts = []
        for h in range(H_RET):
            lg = float(np.log1p(-(2.0 ** (-5 - h))))
            sl = slice(h * DH_RET, (h + 1) * DH_RET)
            q = zr[:, sl]
            k = zr[:, W_RET + h * DH_RET:W_RET + (h + 1) * DH_RET]
            v = zr[:, 2 * W_RET + h * DH_RET:2 * W_RET + (h + 1) * DH_RET]
            gt = zr[:, 3 * W_RET + h * DH_RET:3 * W_RET + (h + 1) * DH_RET]
            q = q * cos + pltpu.roll(q, half, 1) * sin
            k = (k * cos + pltpu.roll(k, half, 1) * sin) * (DH_RET ** -0.5)
            dmask = jnp.where(causal, jnp.exp(lg * jnp.maximum(diff, 0.0)), 0.0)
            s_prev = s_scr[b, h]
            o = _dot(_dot_nt(q, k) * dmask, v)
            o = o + _dot(q * jnp.exp(lg * (idx + 1.0)), s_prev)
            ds = _dot_tn(k * jnp.exp(lg * (L - 1 - idx)), v)
            s_scr[b, h] = float(np.exp(lg * L)) * s_prev + ds
            parts.append(jax.nn.silu(gt) * (_head_norm(o) * gret_ref[:, sl]))
        yr_ref[b] = _bf16(jnp.concatenate(parts, axis=1))

        zm = zm_ref[b]
        xm = zm[:, :W_ML]
        xp = jnp.concatenate([xprev_scr[b], xm], axis=0)
        first = SUBLANES - (CONV_W - 1)
        xc = convb_ref[...] + xp[first:first + L] * convw_ref[0:1, :]
        for j in range(1, CONV_W):
            xc = xc + xp[first + j:first + j + L] * convw_ref[j:j + 1, :]
        xprev_scr[b] = xm[L - SUBLANES:, :]
        cact = jax.nn.silu(xc)
        gates = zif_ref[b] + bif_ref[...]
        fcum = jnp.dot(tril, jax.nn.log_sigmoid(gates), precision=HIGHEST, preferred_element_type=f32)
        gates_t = _pad_rows(gates, LANES).T
        fcum_t = _pad_rows(fcum, LANES).T
        parts = []
        for h in range(H_ML):
            sl = slice(h * DH_ML, (h + 1) * DH_ML)
            ch = cact[:, sl]
            q = _dot(ch, wmq_ref[h])
            k = _dot(ch, wmk_ref[h]) * (DH_ML ** -0.5)
            v = zm[:, W_ML + h * DH_ML:W_ML + (h + 1) * DH_ML]
            o_gate = zm[:, 2 * W_ML + h * DH_ML:2 * W_ML + (h + 1) * DH_ML]
            ig_row, f_row = gates_t[h:h + 1, :L], fcum_t[H_ML + h:H_ML + h + 1, :L]
            ig_col, f_col = gates[:, h:h + 1], fcum[:, H_ML + h:H_ML + h + 1]
            m_prev = m_scr[b, h:h + 1, 0:1]
            c_prev = c_scr[b, h]
            n_prev = n_scr[b, h:h + 1, :]
            log_d = jnp.where(causal, ig_row + f_col - f_row, NEG_INF)
            inter = m_prev + f_col
            m_t = jnp.maximum(inter, jnp.max(log_d, axis=1, keepdims=True))
            a = jnp.exp(inter - m_t)
            s = _dot_nt(q, k) * jnp.exp(log_d - m_t)
            num = _dot(s, v) + a * _dot(q, c_prev)
            den = jnp.sum(s, axis=1, keepdims=True) + a * jnp.sum(q * n_prev, axis=1, keepdims=True)
            hc = num / jnp.maximum(jnp.abs(den), jnp.exp(-m_t))
            m_new = m_t[L - 1:L, :]
            f_last = f_col[L - 1:L, :]
            a_end = jnp.exp(m_prev + f_last - m_new)
            kw = k * jnp.exp(ig_col + f_last - f_col - m_new)
            c_scr[b, h] = a_end * c_prev + _dot_tn(kw, v)
            n_scr[b, h:h + 1, :] = a_end * n_prev + jnp.sum(kw, axis=0, keepdims=True)
            m_scr[b, h:h + 1, :] = jnp.broadcast_to(m_new, (1, LANES))
            parts.append(jax.nn.sigmoid(o_gate) * (_head_norm(hc) * gml_ref[:, sl] + wskip_ref[:, sl] * ch))
        ym_ref[b] = _bf16(jnp.concatenate(parts, axis=1))

    @pl.when(chunk == pl.num_programs(1) - 1)
    def _():
        sret_ref[...] = s_scr[...]
        cml_ref[...] = c_scr[...]
        nml_ref[...] = n_scr[...]
        mml_ref[...] = m_scr[...]
        ctail_ref[...] = xprev_scr[...]


def _rope_tables(pos):
    inv = ROPE_BASE ** (-jnp.arange(DH_RET // 2, dtype=jnp.float32) / (DH_RET // 2))
    ang = pos.astype(jnp.float32)[:, None] * inv[None, :]
    cos, sin = jnp.cos(ang), jnp.sin(ang)
    return jnp.concatenate([cos, cos], axis=1), jnp.concatenate([-sin, sin], axis=1)


def token_mixers(zr, zm, zif, pos, s0, c0, n0, m0, conv_buf,
                 g_ret_gn, w_mq, w_mk, conv_w, conv_b, b_i, b_f, g_ml_gn, w_skip):
    B, T = zr.shape[0], zr.shape[1]
    L = min(CHUNK, T)
    nb = SCAN_BATCH
    cos2, sin2 = _rope_tables(pos)
    f32 = jnp.float32
    m0b = jnp.broadcast_to(m0.astype(f32)[:, :, None], (B, H_ML, LANES))
    cb0 = jnp.pad(conv_buf.astype(f32), ((0, 0), (SUBLANES - (CONV_W - 1), 0), (0, 0)))
    bif = jnp.pad(jnp.concatenate([b_i, b_f]).astype(f32), (0, LANES - 2 * H_ML)).reshape(1, LANES)
    seq = lambda wd: pl.BlockSpec((nb, L, wd), lambda i, c: (i, c, 0))
    tab = pl.BlockSpec((L, DH_RET), lambda i, c: (c, 0))
    st4 = pl.BlockSpec((nb, H_RET, DH_RET, DH_RET), lambda i, c: (i, 0, 0, 0))
    st3 = pl.BlockSpec((nb, H_ML, DH_ML), lambda i, c: (i, 0, 0))
    cb = pl.BlockSpec((nb, SUBLANES, W_ML), lambda i, c: (i, 0, 0))
    row512 = _resident((1, W_ML))
    outs = pl.pallas_call(
        _scan_kernel,
        grid=(B // nb, T // L),
        in_specs=[seq(Z_RET), seq(Z_ML), seq(LANES), tab, tab, st4, st4, st3, st3, cb,
                  row512, _resident((H_ML, DH_ML, DH_ML)), _resident((H_ML, DH_ML, DH_ML)),
                  _resident((CONV_W, W_ML)), row512, _resident((1, LANES)), row512, row512],
        out_specs=[seq(W_RET), seq(W_ML), st4, st4, st3, st3, cb],
        out_shape=[jax.ShapeDtypeStruct((B, T, W_RET), jnp.bfloat16),
                   jax.ShapeDtypeStruct((B, T, W_ML), jnp.bfloat16),
                   jax.ShapeDtypeStruct((B, H_RET, DH_RET, DH_RET), f32),
                   jax.ShapeDtypeStruct((B, H_ML, DH_ML, DH_ML), f32),
                   jax.ShapeDtypeStruct((B, H_ML, DH_ML), f32),
                   jax.ShapeDtypeStruct((B, H_ML, LANES), f32),
                   jax.ShapeDtypeStruct((B, SUBLANES, W_ML), f32)],
        scratch_shapes=[pltpu.VMEM((nb, H_RET, DH_RET, DH_RET), f32),
                        pltpu.VMEM((nb, H_ML, DH_ML, DH_ML), f32),
                        pltpu.VMEM((nb, H_ML, DH_ML), f32),
                        pltpu.VMEM((nb, H_ML, LANES), f32),
                        pltpu.VMEM((nb, SUBLANES, W_ML), f32)],
        compiler_params=pltpu.CompilerParams(dimension_semantics=("arbitrary", "arbitrary"),
                                             vmem_limit_bytes=DENSE_VMEM_LIMIT),
        name="token_mixers",
    )(zr, zm, zif, cos2, sin2, s0.astype(f32), c0.astype(f32), n0.astype(f32), m0b, cb0,
      g_ret_gn.reshape(1, W_RET), _bf16(w_mq), _bf16(w_mk), conv_w, conv_b.reshape(1, W_ML), bif,
      g_ml_gn.reshape(1, W_ML), w_skip.reshape(1, W_ML))
    y_r, y_m, s_new, c_new, n_new, m_new, ctail = outs
    return y_r, y_m, s_new, c_new, n_new, m_new[:, :, 0], ctail[:, SUBLANES - (CONV_W - 1):, :]


def _merge_kernel(x_ref, yr_ref, ym_ref, zg_ref, wur_ref, wum_ref, wo_ref, o_ref):
    zg = zg_ref[...]
    up_r = jnp.dot(yr_ref[...], wur_ref[...], preferred_element_type=jnp.float32)
    up_m = jnp.dot(ym_ref[...], wum_ref[...], preferred_element_type=jnp.float32)
    merged = jax.nn.sigmoid(zg[:, :D_MODEL]) * up_r + jax.nn.sigmoid(zg[:, D_MODEL:]) * up_m
    o_ref[...] = x_ref[...] + jnp.dot(_bf16(merged), wo_ref[...], preferred_element_type=jnp.float32)


def merge_branches(x, y_r, y_m, zg, w_up_r, w_up_m, w_out):
    n = x.shape[0]
    row = lambda i: (i, 0)
    return pl.pallas_call(
        _merge_kernel,
        grid=(n // ROW_BLOCK,),
        in_specs=[pl.BlockSpec((ROW_BLOCK, D_MODEL), row), pl.BlockSpec((ROW_BLOCK, W_RET), row),
                  pl.BlockSpec((ROW_BLOCK, W_ML), row), pl.BlockSpec((ROW_BLOCK, Z_GATE), row),
                  _resident((W_RET, D_MODEL)), _resident((W_ML, D_MODEL)), _resident((D_MODEL, D_MODEL))],
        out_specs=pl.BlockSpec((ROW_BLOCK, D_MODEL), row),
        out_shape=jax.ShapeDtypeStruct((n, D_MODEL), jnp.float32),
        compiler_params=pltpu.CompilerParams(dimension_semantics=("arbitrary",),
                                             vmem_limit_bytes=DENSE_VMEM_LIMIT),
        name="merge_branches",
    )(x, y_r, y_m, zg, _bf16(w_up_r), _bf16(w_up_m), _bf16(w_out))


PEER_SLOTS = PEER_HEADS * PEER_TOPK
ROUTE_TOK_BLOCK = 256
_CAND_GROUPS = ((0, 0), (0, 8), (1, 0)) + tuple((a, 0) for a in range(2, 8))


def _take_top16_rows(s, code):
    big = jnp.float32(1e9)
    vals, codes = [], []
    for _ in range(PEER_TOPK):
        m = jnp.max(s, axis=0, keepdims=True)
        c = jnp.min(jnp.where(s == m, code, big), axis=0, keepdims=True)
        s = jnp.where(code == c, NEG_INF, s)
        vals.append(m)
        codes.append(c)
    return jnp.concatenate(vals, axis=0), jnp.concatenate(codes, axis=0)


def _lookup_rows(table, sel):
    out = jnp.zeros_like(sel)
    for a in range(PEER_TOPK):
        out = jnp.where(sel == jnp.float32(a), table[a:a + 1, :], out)
    return out


def _route_one_head(s1, s2):
    key_id = lax.broadcasted_iota(jnp.int32, (N_KEYS, LANES), 0).astype(jnp.float32)
    v1, i1 = _take_top16_rows(s1, key_id)
    v2, i2 = _take_top16_rows(s2, key_id)
    sub = lax.broadcasted_iota(jnp.int32, (SUBLANES, LANES), 0)
    subf = sub.astype(jnp.float32)
    cand, code = [], []
    for a, b0 in _CAND_GROUPS:
        c = v1[a:a + 1, :] + v2[b0:b0 + SUBLANES, :]
        n_valid = PEER_TOPK // (a + 1) - b0
        cand.append(c if n_valid >= SUBLANES else jnp.where(sub < n_valid, c, NEG_INF))
        code.append(subf + jnp.float32(a * PEER_TOPK + b0))
    cand.append(v1[SUBLANES:, :] + v2[0:1, :])
    code.append((subf + jnp.float32(SUBLANES)) * jnp.float32(PEER_TOPK))
    sc, cc = _take_top16_rows(jnp.concatenate(cand, axis=0), jnp.concatenate(code, axis=0))
    a_sel = jnp.floor(cc * jnp.float32(1.0 / PEER_TOPK))
    b_sel = cc - a_sel * jnp.float32(PEER_TOPK)
    e = _lookup_rows(i1, a_sel) * jnp.float32(N_KEYS) + _lookup_rows(i2, b_sel)
    ex = jnp.exp(sc - sc[0:1, :])
    gate = ex / jnp.sum(ex, axis=0, keepdims=True)
    return e.astype(jnp.int32), gate


def _peer_route_kernel(x_ref, g_ref, wpq_ref, keys_ref, xn_ref, e_ref, gate_ref, q_scr):
    xn = _rms(x_ref[...], g_ref[...])
    xn_ref[...] = xn
    q = jnp.dot(_bf16(xn), wpq_ref[...], preferred_element_type=jnp.float32)
    for c in range(2 * PEER_HEADS):
        q_scr[c] = _bf16(q[:, c * PEER_DQ_HALF:(c + 1) * PEER_DQ_HALF])

    def head_body(n, carry):
        rows = pl.ds(pl.multiple_of(n * PEER_TOPK, PEER_TOPK), PEER_TOPK)
        for lg in range(ROUTE_TOK_BLOCK // LANES):
            toks = slice(lg * LANES, (lg + 1) * LANES)
            s = [lax.dot_general(keys_ref[2 * n + h], q_scr[2 * n + h, toks, :], NT_DIMS,
                                 preferred_element_type=jnp.float32) for h in range(2)]
            e, gate = _route_one_head(s[0], s[1])
            e_ref[rows, toks] = e
            gate_ref[rows, toks] = gate
        return carry

    lax.fori_loop(0, PEER_HEADS, head_body, 0)


def peer_route(x, g_ffn, w_pq, peer_keys):
    n = x.shape[0]
    tok = lambda i: (i, 0)
    slots = pl.BlockSpec((PEER_SLOTS, ROUTE_TOK_BLOCK), lambda i: (0, i))
    return pl.pallas_call(
        _peer_route_kernel,
        grid=(n // ROUTE_TOK_BLOCK,),
        in_specs=[pl.BlockSpec((ROUTE_TOK_BLOCK, D_MODEL), tok),
                  _resident((1, D_MODEL)),
                  _resident((D_MODEL, PEER_HEADS * PEER_DQ)),
                  _resident((2 * PEER_HEADS, N_KEYS, PEER_DQ_HALF))],
        out_specs=[pl.BlockSpec((ROUTE_TOK_BLOCK, D_MODEL), tok), slots, slots],
        out_shape=[jax.ShapeDtypeStruct((n, D_MODEL), jnp.float32),
                   jax.ShapeDtypeStruct((PEER_SLOTS, n), jnp.int32),
                   jax.ShapeDtypeStruct((PEER_SLOTS, n), jnp.float32)],
        scratch_shapes=[pltpu.VMEM((2 * PEER_HEADS, ROUTE_TOK_BLOCK, PEER_DQ_HALF), jnp.bfloat16)],
        compiler_params=pltpu.CompilerParams(dimension_semantics=("arbitrary",),
                                             vmem_limit_bytes=DENSE_VMEM_LIMIT),
        name="peer_route",
    )(x, g_ffn.reshape(1, D_MODEL), _bf16(w_pq),
      _bf16(peer_keys.reshape(2 * PEER_HEADS, N_KEYS, PEER_DQ_HALF)))


EXPERT_ROWS = SUBLANES // 2
CHUNKS = D_MODEL // LANES
CHUNKS_LOG2 = CHUNKS.bit_length() - 1
PEER_K = PEER_SLOTS * CHUNKS
PEER_TOK_BLOCK = 256
PEER_TOK_UNROLL = 8
PACK_ROW_BLOCK = 4096
PEER_TABLE_BYTES = N_EXPERTS * EXPERT_ROWS * LANES * 4
PEER_VMEM_LIMIT = PEER_TABLE_BYTES + (20 << 20)


def _peer_pack_kernel(x_ref, o_ref):
    f32 = jnp.float32
    half = PACK_ROW_BLOCK // 2
    lo = pltpu.bitcast(_bf16(x_ref[pl.ds(0, half, stride=2), :]).astype(f32), jnp.uint32)
    hi = pltpu.bitcast(_bf16(x_ref[pl.ds(1, half, stride=2), :]).astype(f32), jnp.uint32)
    o_ref[...] = (hi & jnp.uint32(0xFFFF0000)) | (lo >> 16)


def peer_pack_table(table):
    t2 = table.reshape(N_EXPERTS * CHUNKS, LANES)
    return pl.pallas_call(
        _peer_pack_kernel,
        grid=(N_EXPERTS * CHUNKS // PACK_ROW_BLOCK,),
        in_specs=[pl.BlockSpec((PACK_ROW_BLOCK, LANES), lambda i: (i, 0))],
        out_specs=pl.BlockSpec((PACK_ROW_BLOCK // 2, LANES), lambda i: (i, 0)),
        out_shape=jax.ShapeDtypeStruct((N_EXPERTS * EXPERT_ROWS, LANES), jnp.uint32),
        name="peer_pack",
    )(t2)


def _slot_expander():
    j = lax.broadcasted_iota(jnp.int32, (PEER_SLOTS, PEER_K), 0)
    c = lax.broadcasted_iota(jnp.int32, (PEER_SLOTS, PEER_K), 1)
    return jnp.where((c >> CHUNKS_LOG2) == j, 1.0, 0.0)


def _to_token_major(a):
    return jnp.concatenate([a[:, g * LANES:(g + 1) * LANES].T for g in range(a.shape[1] // LANES)], axis=0)


def _chunk_diagonal():
    s_out = lax.broadcasted_iota(jnp.int32, (SUBLANES, PEER_K), 0)
    c = lax.broadcasted_iota(jnp.int32, (SUBLANES, PEER_K), 1)
    return (c & (CHUNKS - 1)) == s_out


def _gather_tiles(row_ref, tab_ref, t):
    tiles = []
    for j in range(0, PEER_SLOTS, 2):
        pair = [tab_ref[pl.ds(pl.multiple_of(row_ref[t, j + k], EXPERT_ROWS), EXPERT_ROWS), :] for k in range(2)]
        tiles.append(pltpu.bitcast(jnp.concatenate(pair, axis=0), jnp.bfloat16))
    return jnp.concatenate(tiles, axis=0)


def _split_bf16_rows(a):
    hi = _bf16(a).astype(jnp.float32)
    return _bf16(jnp.concatenate([hi, a - hi], axis=0))


def _peer_u_kernel(row_ref, x_ref, tab_ref, act_ref, r_scr):
    diag = _chunk_diagonal()

    def tok_body(t, carry):
        m = _gather_tiles(row_ref, tab_ref, t)
        res = lax.dot_general(_split_bf16_rows(x_ref[t]), m, NT_DIMS, preferred_element_type=jnp.float32)
        r = jnp.where(diag, res[:SUBLANES] + res[SUBLANES:], 0.0)
        r_scr[pl.ds(t, 1), :] = jnp.sum(r, axis=0, keepdims=True)
        return carry

    lax.fori_loop(0, PEER_TOK_BLOCK, tok_body, 0, unroll=PEER_TOK_UNROLL)
    act_ref[...] = lax.dot_general(_slot_expander(), r_scr[...], NT_DIMS, precision=HIGHEST,
                                   preferred_element_type=jnp.float32)


def _peer_v_kernel(row_ref, act_ref, gate_ref, tab_ref, out_ref, whi_scr, wlo_scr):
    act = act_ref[...]
    w = _to_token_major(gate_ref[...] * (0.5 * act * (1.0 + lax.erf(act * (2.0 ** -0.5)))))
    expand = _bf16(_slot_expander())
    w_hi = _bf16(w)
    w_lo = _bf16(w - w_hi.astype(jnp.float32))
    whi_scr[...] = jnp.dot(w_hi, expand, preferred_element_type=jnp.float32)
    wlo_scr[...] = jnp.dot(w_lo, expand, preferred_element_type=jnp.float32)
    diag = _chunk_diagonal()

    def tok_body(t, carry):
        m = _gather_tiles(row_ref, tab_ref, t)
        lhs = jnp.concatenate([jnp.where(diag, whi_scr[pl.ds(t, 1), :], 0.0),
                               jnp.where(diag, wlo_scr[pl.ds(t, 1), :], 0.0)], axis=0)
        res = jnp.dot(_bf16(lhs), m, preferred_element_type=jnp.float32)
        out_ref[t] = res[:SUBLANES] + res[SUBLANES:]
        return carry

    lax.fori_loop(0, PEER_TOK_BLOCK, tok_body, 0, unroll=PEER_TOK_UNROLL)


_PEER_PARAMS = pltpu.CompilerParams(dimension_semantics=("arbitrary",), vmem_limit_bytes=PEER_VMEM_LIMIT)


def peer_experts(xn, e_t, g_t, peer_u, peer_v):
    n = xn.shape[0]
    rows = (e_t * EXPERT_ROWS).T
    smem_rows = pl.BlockSpec((PEER_TOK_BLOCK, PEER_SLOTS), lambda i: (i, 0), memory_space=pltpu.SMEM)
    slots = pl.BlockSpec((PEER_SLOTS, PEER_TOK_BLOCK), lambda i: (0, i))
    tiles = pl.BlockSpec((PEER_TOK_BLOCK, SUBLANES, LANES), lambda i: (i, 0, 0))
    table = _resident((N_EXPERTS * EXPERT_ROWS, LANES))
    wide = pltpu.VMEM((PEER_TOK_BLOCK, PEER_K), jnp.float32)
    act = pl.pallas_call(
        _peer_u_kernel,
        grid=(n // PEER_TOK_BLOCK,),
        in_specs=[smem_rows, tiles, table],
        out_specs=slots,
        out_shape=jax.ShapeDtypeStruct((PEER_SLOTS, n), jnp.float32),
        scratch_shapes=[wide],
        compiler_params=_PEER_PARAMS,
        name="peer_u",
    )(rows, xn.reshape(n, SUBLANES, LANES), peer_pack_table(peer_u))
    out = pl.pallas_call(
        _peer_v_kernel,
        grid=(n // PEER_TOK_BLOCK,),
        in_specs=[smem_rows, slots, slots, table],
        out_specs=tiles,
        out_shape=jax.ShapeDtypeStruct((n, SUBLANES, LANES), jnp.float32),
        scratch_shapes=[wide, wide],
        compiler_params=_PEER_PARAMS,
        name="peer_v",
    )(rows, act, g_t, peer_pack_table(peer_v))
    return out.reshape(n, D_MODEL)


def _ple_kernel(x_ref, o_ref, p_ref, gple_ref, wpg_ref, wple_ref, gfin_ref, y_ref):
    x = x_ref[...] + o_ref[...]
    gate = jax.nn.sigmoid(_dot(_rms(x, gple_ref[...]), wpg_ref[...]))
    x = x + _dot(p_ref[...], wple_ref[...]) * gate
    y_ref[...] = _rms(x, gfin_ref[...])


def ple_and_final_norm(x, peer_out, p, g_ple, w_pg, w_ple, g_final):
    n = x.shape[0]
    row = lambda i: (i, 0)
    blk = pl.BlockSpec((ROW_BLOCK, D_MODEL), row)
    return pl.pallas_call(
        _ple_kernel,
        grid=(n // ROW_BLOCK,),
        in_specs=[blk, blk, pl.BlockSpec((ROW_BLOCK, D_PLE), row), _resident((1, D_MODEL)),
                  _resident((D_MODEL, D_MODEL)), _resident((D_PLE, D_MODEL)), _resident((1, D_MODEL))],
        out_specs=blk,
        out_shape=jax.ShapeDtypeStruct((n, D_MODEL), jnp.float32),
        compiler_params=pltpu.CompilerParams(dimension_semantics=("arbitrary",),
                                             vmem_limit_bytes=DENSE_VMEM_LIMIT),
        name="ple_final",
    )(x, peer_out, p, g_ple.reshape(1, D_MODEL), _bf16(w_pg), _bf16(w_ple), g_final.reshape(1, D_MODEL))


def kernel(x_prompt, x_sample, p_prompt, p_sample, state_ret, state_mlstm_C, state_mlstm_n,
           state_mlstm_m, state_conv, g_mix, w_in, g_ret_gn, w_mq, w_mk, conv_w, conv_b, b_i, b_f,
           g_ml_gn, w_skip, w_up_r, w_up_m, w_out, g_ffn, w_pq, peer_keys, peer_u, peer_v,
           g_ple, w_pg, w_ple, g_final):
    f32 = jnp.float32
    Bp = x_prompt.shape[0]
    pos_p = jnp.arange(x_prompt.shape[1], dtype=jnp.int32)
    pos_s = PAST_LEN + jnp.arange(x_sample.shape[1], dtype=jnp.int32)
    z_ret = jnp.zeros((Bp, H_RET, DH_RET, DH_RET), f32)
    z_c = jnp.zeros((Bp, H_ML, DH_ML, DH_ML), f32)
    z_n = jnp.zeros((Bp, H_ML, DH_ML), f32)
    z_m = jnp.zeros((Bp, H_ML), f32)
    z_buf = jnp.zeros((Bp, CONV_W - 1, W_ML), x_prompt.dtype)
    mixer_w = (g_ret_gn[0], w_mq[0], w_mk[0], conv_w[0], conv_b[0], b_i[0], b_f[0], g_ml_gn[0], w_skip[0])

    def stream(x, pos, s0, c0, n0, m0, cbuf):
        B, T, _ = x.shape
        x2 = x.reshape(B * T, D_MODEL)
        zr, zm, zif, zg = in_proj(x2, g_mix[0], w_in[0])
        seq = lambda z: z.reshape(B, T, z.shape[-1])
        y_r, y_m, *states = token_mixers(seq(zr), seq(zm), seq(zif), pos, s0, c0, n0, m0, cbuf, *mixer_w)
        x1 = merge_branches(x2, y_r.reshape(B * T, W_RET), y_m.reshape(B * T, W_ML), zg,
                            w_up_r[0], w_up_m[0], w_out[0])
        return x1, states

    hp, st_p = stream(x_prompt, pos_p, z_ret, z_c, z_n, z_m, z_buf)
    hs, st_s = stream(x_sample, pos_s, state_ret[0], state_mlstm_C[0], state_mlstm_n[0],
                      state_mlstm_m[0], state_conv[0])
    n_p = hp.shape[0]
    x_all = jnp.concatenate([hp, hs], axis=0)
    p_all = jnp.concatenate([p_prompt[0].reshape(n_p, D_PLE), p_sample[0].reshape(-1, D_PLE)], axis=0)
    xn, e_t, g_t = peer_route(x_all, g_ffn[0], w_pq[0], peer_keys[0])
    peer_out = peer_experts(xn, e_t, g_t, peer_u[0], peer_v[0])
    y_all = ple_and_final_norm(x_all, peer_out, p_all, g_ple[0], w_pg[0], w_ple[0], g_final)
    y_prompt = y_all[:n_p].reshape(x_prompt.shape)
    y_sample = y_all[n_p:].reshape(x_sample.shape)
    return (y_prompt, y_sample, *[a[None] for a in st_p], *[a[None] for a in st_s])
```

```python
import jax, jax.numpy as jnp
from jax import lax
import numpy as np
from jax.experimental import pallas as pl
from jax.experimental.pallas import tpu as pltpu

D_MODEL = 1024
PAST_LEN = 2048
CHUNK = 64
EPS = 1e-6
ROPE_BASE = 10000.0
H_RET = 4
W_RET = D_MODEL // 2
DH_RET = W_RET // H_RET
H_ML = 4
W_ML = D_MODEL // 2
DH_ML = W_ML // H_ML
CONV_W = 4
PEER_HEADS = 8
N_KEYS = 128
N_EXPERTS = N_KEYS * N_KEYS
PEER_TOPK = 16
PEER_DQ = 256
PEER_DQ_HALF = PEER_DQ // 2
D_PLE = 256

SUBLANES = 8
LANES = 128
NEG_INF = float("-inf")
HIGHEST = lax.Precision.HIGHEST
NT_DIMS = (((1,), (1,)), ((), ()))


def _bf16(a):
    return a.astype(jnp.bfloat16)


def _dot(a, b):
    return jnp.dot(_bf16(a), _bf16(b), preferred_element_type=jnp.float32)


def _dot_nt(a, b):
    return lax.dot_general(_bf16(a), _bf16(b), NT_DIMS, preferred_element_type=jnp.float32)


def _pad_rows(a, rows):
    return jnp.concatenate([a, jnp.zeros((rows - a.shape[0], a.shape[1]), a.dtype)], axis=0)


def _dot_tn(a, b):
    return _dot(_pad_rows(a, LANES).T, _pad_rows(b, LANES))


def _rms(x, g):
    return x * lax.rsqrt(jnp.mean(x * x, axis=-1, keepdims=True) + EPS) * g


def _resident(shape):
    return pl.BlockSpec(shape, lambda *_: (0,) * len(shape), pipeline_mode=pl.Buffered(1))


ROW_BLOCK = 256
DENSE_VMEM_LIMIT = 48 << 20
Z_RET = 4 * W_RET
Z_ML = 3 * W_ML
Z_GATE = 2 * D_MODEL
SCAN_BATCH = 4


def _in_proj_kernel(x_ref, g_ref, wr_ref, wm_ref, wif_ref, wg_ref, zr_ref, zm_ref, zif_ref, zg_ref):
    h = _bf16(_rms(x_ref[...], g_ref[...]))
    for w_ref, z_ref in ((wr_ref, zr_ref), (wm_ref, zm_ref), (wif_ref, zif_ref), (wg_ref, zg_ref)):
        z_ref[...] = jnp.dot(h, w_ref[...], preferred_element_type=jnp.float32)


def in_proj(x, g_mix, w_in):
    n = x.shape[0]
    w = _bf16(w_in)
    c_if = Z_RET + Z_ML
    w_if = jnp.pad(w[:, c_if:c_if + 2 * H_ML], ((0, 0), (0, LANES - 2 * H_ML)))
    widths = (Z_RET, Z_ML, LANES, Z_GATE)
    row = lambda i: (i, 0)
    return pl.pallas_call(
        _in_proj_kernel,
        grid=(n // ROW_BLOCK,),
        in_specs=[pl.BlockSpec((ROW_BLOCK, D_MODEL), row), _resident((1, D_MODEL))]
        + [_resident((D_MODEL, wd)) for wd in widths],
        out_specs=[pl.BlockSpec((ROW_BLOCK, wd), row) for wd in widths],
        out_shape=[jax.ShapeDtypeStruct((n, wd), jnp.float32) for wd in widths],
        compiler_params=pltpu.CompilerParams(dimension_semantics=("arbitrary",),
                                             vmem_limit_bytes=DENSE_VMEM_LIMIT),
        name="in_proj",
    )(x, g_mix.reshape(1, D_MODEL), w[:, :Z_RET], w[:, Z_RET:c_if], w_if, w[:, c_if + 2 * H_ML:])


def _head_norm(y):
    return y * lax.rsqrt(jnp.mean(y * y, axis=-1, keepdims=True) + EPS)


def _scan_kernel(zr_ref, zm_ref, zif_ref, cos_ref, sin_ref, s0_ref, c0_ref, n0_ref, m0_ref, cb0_ref,
                 gret_ref, wmq_ref, wmk_ref, convw_ref, convb_ref, bif_ref, gml_ref, wskip_ref,
                 yr_ref, ym_ref, sret_ref, cml_ref, nml_ref, mml_ref, ctail_ref,
                 s_scr, c_scr, n_scr, m_scr, xprev_scr):
    nb, L = zr_ref.shape[0], zr_ref.shape[1]
    chunk = pl.program_id(1)

    @pl.when(chunk == 0)
    def _():
        s_scr[...] = s0_ref[...]
        c_scr[...] = c0_ref[...]
        n_scr[...] = n0_ref[...]
        m_scr[...] = m0_ref[...]
        xprev_scr[...] = cb0_ref[...]

    f32 = jnp.float32
    row = lax.broadcasted_iota(jnp.int32, (L, L), 0)
    col = lax.broadcasted_iota(jnp.int32, (L, L), 1)
    causal = row >= col
    diff = (row - col).astype(f32)
    tril = jnp.where(causal, 1.0, 0.0).astype(f32)
    idx = lax.broadcasted_iota(jnp.int32, (L, 1), 0).astype(f32)
    cos, sin = cos_ref[...], sin_ref[...]
    half = DH_RET // 2

    for b in range(nb):
        zr = zr_ref[b]
        parts = []
        for h in range(H_RET):
            lg = float(np.log1p(-(2.0 ** (-5 - h))))
            sl = slice(h * DH_RET, (h + 1) * DH_RET)
            q = zr[:, sl]
            k = zr[:, W_RET + h * DH_RET:W_RET + (h + 1) * DH_RET]
            v = zr[:, 2 * W_RET + h * DH_RET:2 * W_RET + (h + 1) * DH_RET]
            gt = zr[:, 3 * W_RET + h * DH_RET:3 * W_RET + (h + 1) * DH_RET]
            q = q * cos + pltpu.roll(q, half, 1) * sin
            k = (k * cos + pltpu.roll(k, half, 1) * sin) * (DH_RET ** -0.5)
            dmask = jnp.where(causal, jnp.exp(lg * jnp.maximum(diff, 0.0)), 0.0)
            s_prev = s_scr[b, h]
            o = _dot(_dot_nt(q, k) * dmask, v)
            o = o + _dot(q * jnp.exp(lg * (idx + 1.0)), s_prev)
            ds = _dot_tn(k * jnp.exp(lg * (L - 1 - idx)), v)
            s_scr[b, h] = float(np.exp(lg * L)) * s_prev + ds
            parts.append(jax.nn.silu(gt) * (_head_norm(o) * gret_ref[:, sl]))
        yr_ref[b] = _bf16(jnp.concatenate(parts, axis=1))

        zm = zm_ref[b]
        xm = zm[:, :W_ML]
        xp = jnp.concatenate([xprev_scr[b], xm], axis=0)
        first = SUBLANES - (CONV_W - 1)
        xc = convb_ref[...] + xp[first:first + L] * convw_ref[0:1, :]
        for j in range(1, CONV_W):
            xc = xc + xp[first + j:first + j + L] * convw_ref[j:j + 1, :]
        xprev_scr[b] = xm[L - SUBLANES:, :]
        cact = jax.nn.silu(xc)
        gates = zif_ref[b] + bif_ref[...]
        fcum = jnp.dot(tril, jax.nn.log_sigmoid(gates), precision=HIGHEST, preferred_element_type=f32)
        gates_t = _pad_rows(gates, LANES).T
        fcum_t = _pad_rows(fcum, LANES).T
        parts = []
        for h in range(H_ML):
            sl = slice(h * DH_ML, (h + 1) * DH_ML)
            ch = cact[:, sl]
            q = _dot(ch, wmq_ref[h])
            k = _dot(ch, wmk_ref[h]) * (DH_ML ** -0.5)
            v = zm[:, W_ML + h * DH_ML:W_ML + (h + 1) * DH_ML]
            o_gate = zm[:, 2 * W_ML + h * DH_ML:2 * W_ML + (h + 1) * DH_ML]
            ig_row, f_row = gates_t[h:h + 1, :L], fcum_t[H_ML + h:H_ML + h + 1, :L]
            ig_col, f_col = gates[:, h:h + 1], fcum[:, H_ML + h:H_ML + h + 1]
            m_prev = m_scr[b, h:h + 1, 0:1]
            c_prev = c_scr[b, h]
            n_prev = n_scr[b, h:h + 1, :]
            log_d = jnp.where(causal, ig_row + f_col - f_row, NEG_INF)
            inter = m_prev + f_col
            m_t = jnp.maximum(inter, jnp.max(log_d, axis=1, keepdims=True))
            a = jnp.exp(inter - m_t)
            s = _dot_nt(q, k) * jnp.exp(log_d - m_t)
            num = _dot(s, v) + a * _dot(q, c_prev)
            den = jnp.sum(s, axis=1, keepdims=True) + a * jnp.sum(q * n_prev, axis=1, keepdims=True)
            hc = num / jnp.maximum(jnp.abs(den), jnp.exp(-m_t))
            m_new = m_t[L - 1:L, :]
            f_last = f_col[L - 1:L, :]
            a_end = jnp.exp(m_prev + f_last - m_new)
            kw = k * jnp.exp(ig_col + f_last - f_col - m_new)
            c_scr[b, h] = a_end * c_prev + _dot_tn(kw, v)
            n_scr[b, h:h + 1, :] = a_end * n_prev + jnp.sum(kw, axis=0, keepdims=True)
            m_scr[b, h:h + 1, :] = jnp.broadcast_to(m_new, (1, LANES))
            parts.append(jax.nn.sigmoid(o_gate) * (_head_norm(hc) * gml_ref[:, sl] + wskip_ref[:, sl] * ch))
        ym_ref[b] = _bf16(jnp.concatenate(parts, axis=1))

    @pl.when(chunk == pl.num_programs(1) - 1)
    def _():
        sret_ref[...] = s_scr[...]
        cml_ref[...] = c_scr[...]
        nml_ref[...] = n_scr[...]
        mml_ref[...] = m_scr[...]
        ctail_ref[...] = xprev_scr[...]


def _rope_tables(pos):
    inv = ROPE_BASE ** (-jnp.arange(DH_RET // 2, dtype=jnp.float32) / (DH_RET // 2))
    ang = pos.astype(jnp.float32)[:, None] * inv[None, :]
    cos, sin = jnp.cos(ang), jnp.sin(ang)
    return jnp.concatenate([cos, cos], axis=1), jnp.concatenate([-sin, sin], axis=1)


def token_mixers(zr, zm, zif, pos, s0, c0, n0, m0, conv_buf,
                 g_ret_gn, w_mq, w_mk, conv_w, conv_b, b_i, b_f, g_ml_gn, w_skip):
    B, T = zr.shape[0], zr.shape[1]
    L = min(CHUNK, T)
    nb = SCAN_BATCH
    cos2, sin2 = _rope_tables(pos)
    f32 = jnp.float32
    m0b = jnp.broadcast_to(m0.astype(f32)[:, :, None], (B, H_ML, LANES))
    cb0 = jnp.pad(conv_buf.astype(f32), ((0, 0), (SUBLANES - (CONV_W - 1), 0), (0, 0)))
    bif = jnp.pad(jnp.concatenate([b_i, b_f]).astype(f32), (0, LANES - 2 * H_ML)).reshape(1, LANES)
    seq = lambda wd: pl.BlockSpec((nb, L, wd), lambda i, c: (i, c, 0))
    tab = pl.BlockSpec((L, DH_RET), lambda i, c: (c, 0))
    st4 = pl.BlockSpec((nb, H_RET, DH_RET, DH_RET), lambda i, c: (i, 0, 0, 0))
    st3 = pl.BlockSpec((nb, H_ML, DH_ML), lambda i, c: (i, 0, 0))
    cb = pl.BlockSpec((nb, SUBLANES, W_ML), lambda i, c: (i, 0, 0))
    row512 = _resident((1, W_ML))
    outs = pl.pallas_call(
        _scan_kernel,
        grid=(B // nb, T // L),
        in_specs=[seq(Z_RET), seq(Z_ML), seq(LANES), tab, tab, st4, st4, st3, st3, cb,
                  row512, _resident((H_ML, DH_ML, DH_ML)), _resident((H_ML, DH_ML, DH_ML)),
                  _resident((CONV_W, W_ML)), row512, _resident((1, LANES)), row512, row512],
        out_specs=[seq(W_RET), seq(W_ML), st4, st4, st3, st3, cb],
        out_shape=[jax.ShapeDtypeStruct((B, T, W_RET), jnp.bfloat16),
                   jax.ShapeDtypeStruct((B, T, W_ML), jnp.bfloat16),
                   jax.ShapeDtypeStruct((B, H_RET, DH_RET, DH_RET), f32),
                   jax.ShapeDtypeStruct((B, H_ML, DH_ML, DH_ML), f32),
                   jax.ShapeDtypeStruct((B, H_ML, DH_ML), f32),
                   jax.ShapeDtypeStruct((B, H_ML, LANES), f32),
                   jax.ShapeDtypeStruct((B, SUBLANES, W_ML), f32)],
        scratch_shapes=[pltpu.VMEM((nb, H_RET, DH_RET, DH_RET), f32),
                        pltpu.VMEM((nb, H_ML, DH_ML, DH_ML), f32),
                        pltpu.VMEM((nb, H_ML, DH_ML), f32),
                        pltpu.VMEM((nb, H_ML, LANES), f32),
                        pltpu.VMEM((nb, SUBLANES, W_ML), f32)],
        compiler_params=pltpu.CompilerParams(dimension_semantics=("arbitrary", "arbitrary"),
                                             vmem_limit_bytes=DENSE_VMEM_LIMIT),
        name="token_mixers",
    )(zr, zm, zif, cos2, sin2, s0.astype(f32), c0.astype(f32), n0.astype(f32), m0b, cb0,
      g_ret_gn.reshape(1, W_RET), _bf16(w_mq), _bf16(w_mk), conv_w, conv_b.reshape(1, W_ML), bif,
      g_ml_gn.reshape(1, W_ML), w_skip.reshape(1, W_ML))
    y_r, y_m, s_new, c_new, n_new, m_new, ctail = outs
    return y_r, y_m, s_new, c_new, n_new, m_new[:, :, 0], ctail[:, SUBLANES - (CONV_W - 1):, :]


def _merge_kernel(x_ref, yr_ref, ym_ref, zg_ref, wur_ref, wum_ref, wo_ref, o_ref):
    zg = zg_ref[...]
    up_r = jnp.dot(yr_ref[...], wur_ref[...], preferred_element_type=jnp.float32)
    up_m = jnp.dot(ym_ref[...], wum_ref[...], preferred_element_type=jnp.float32)
    merged = jax.nn.sigmoid(zg[:, :D_MODEL]) * up_r + jax.nn.sigmoid(zg[:, D_MODEL:]) * up_m
    o_ref[...] = x_ref[...] + jnp.dot(_bf16(merged), wo_ref[...], preferred_element_type=jnp.float32)


def merge_branches(x, y_r, y_m, zg, w_up_r, w_up_m, w_out):
    n = x.shape[0]
    row = lambda i: (i, 0)
    return pl.pallas_call(
        _merge_kernel,
        grid=(n // ROW_BLOCK,),
        in_specs=[pl.BlockSpec((ROW_BLOCK, D_MODEL), row), pl.BlockSpec((ROW_BLOCK, W_RET), row),
                  pl.BlockSpec((ROW_BLOCK, W_ML), row), pl.BlockSpec((ROW_BLOCK, Z_GATE), row),
                  _resident((W_RET, D_MODEL)), _resident((W_ML, D_MODEL)), _resident((D_MODEL, D_MODEL))],
        out_specs=pl.BlockSpec((ROW_BLOCK, D_MODEL), row),
        out_shape=jax.ShapeDtypeStruct((n, D_MODEL), jnp.float32),
        compiler_params=pltpu.CompilerParams(dimension_semantics=("arbitrary",),
                                             vmem_limit_bytes=DENSE_VMEM_LIMIT),
        name="merge_branches",
    )(x, y_r, y_m, zg, _bf16(w_up_r), _bf16(w_up_m), _bf16(w_out))


PEER_SLOTS = PEER_HEADS * PEER_TOPK
ROUTE_TOK_BLOCK = 256
_CAND_GROUPS = ((0, 0), (0, 8), (1, 0)) + tuple((a, 0) for a in range(2, 8))


def _take_top16_rows(s, code):
    big = jnp.float32(1e9)
    vals, codes = [], []
    for _ in range(PEER_TOPK):
        m = jnp.max(s, axis=0, keepdims=True)
        c = jnp.min(jnp.where(s == m, code, big), axis=0, keepdims=True)
        s = jnp.where(code == c, NEG_INF, s)
        vals.append(m)
        codes.append(c)
    return jnp.concatenate(vals, axis=0), jnp.concatenate(codes, axis=0)


def _lookup_rows(table, sel):
    out = jnp.zeros_like(sel)
    for a in range(PEER_TOPK):
        out = jnp.where(sel == jnp.float32(a), table[a:a + 1, :], out)
    return out


def _route_one_head(s1, s2):
    key_id = lax.broadcasted_iota(jnp.int32, (N_KEYS, LANES), 0).astype(jnp.float32)
    v1, i1 = _take_top16_rows(s1, key_id)
    v2, i2 = _take_top16_rows(s2, key_id)
    sub = lax.broadcasted_iota(jnp.int32, (SUBLANES, LANES), 0)
    subf = sub.astype(jnp.float32)
    cand, code = [], []
    for a, b0 in _CAND_GROUPS:
        c = v1[a:a + 1, :] + v2[b0:b0 + SUBLANES, :]
        n_valid = PEER_TOPK // (a + 1) - b0
        cand.append(c if n_valid >= SUBLANES else jnp.where(sub < n_valid, c, NEG_INF))
        code.append(subf + jnp.float32(a * PEER_TOPK + b0))
    cand.append(v1[SUBLANES:, :] + v2[0:1, :])
    code.append((subf + jnp.float32(SUBLANES)) * jnp.float32(PEER_TOPK))
    sc, cc = _take_top16_rows(jnp.concatenate(cand, axis=0), jnp.concatenate(code, axis=0))
    a_sel = jnp.floor(cc * jnp.float32(1.0 / PEER_TOPK))
    b_sel = cc - a_sel * jnp.float32(PEER_TOPK)
    e = _lookup_rows(i1, a_sel) * jnp.float32(N_KEYS) + _lookup_rows(i2, b_sel)
    ex = jnp.exp(sc - sc[0:1, :])
    gate = ex / jnp.sum(ex, axis=0, keepdims=True)
    return e.astype(jnp.int32), gate


def _peer_route_kernel(x_ref, g_ref, wpq_ref, keys_ref, xn_ref, e_ref, gate_ref, q_scr):
    xn = _rms(x_ref[...], g_ref[...])
    xn_ref[...] = xn
    q = jnp.dot(_bf16(xn), wpq_ref[...], preferred_element_type=jnp.float32)
    for c in range(2 * PEER_HEADS):
        q_scr[c] = _bf16(q[:, c * PEER_DQ_HALF:(c + 1) * PEER_DQ_HALF])

    def head_body(n, carry):
        rows = pl.ds(pl.multiple_of(n * PEER_TOPK, PEER_TOPK), PEER_TOPK)
        for lg in range(ROUTE_TOK_BLOCK // LANES):
            toks = slice(lg * LANES, (lg + 1) * LANES)
            s = [lax.dot_general(keys_ref[2 * n + h], q_scr[2 * n + h, toks, :], NT_DIMS,
                                 preferred_element_type=jnp.float32) for h in range(2)]
            e, gate = _route_one_head(s[0], s[1])
            e_ref[rows, toks] = e
            gate_ref[rows, toks] = gate
        return carry

    lax.fori_loop(0, PEER_HEADS, head_body, 0)


def peer_route(x, g_ffn, w_pq, peer_keys):
    n = x.shape[0]
    tok = lambda i: (i, 0)
    slots = pl.BlockSpec((PEER_SLOTS, ROUTE_TOK_BLOCK), lambda i: (0, i))
    return pl.pallas_call(
        _peer_route_kernel,
        grid=(n // ROUTE_TOK_BLOCK,),
        in_specs=[pl.BlockSpec((ROUTE_TOK_BLOCK, D_MODEL), tok),
                  _resident((1, D_MODEL)),
                  _resident((D_MODEL, PEER_HEADS * PEER_DQ)),
                  _resident((2 * PEER_HEADS, N_KEYS, PEER_DQ_HALF))],
        out_specs=[pl.BlockSpec((ROUTE_TOK_BLOCK, D_MODEL), tok), slots, slots],
        out_shape=[jax.ShapeDtypeStruct((n, D_MODEL), jnp.float32),
                   jax.ShapeDtypeStruct((PEER_SLOTS, n), jnp.int32),
                   jax.ShapeDtypeStruct((PEER_SLOTS, n), jnp.float32)],
        scratch_shapes=[pltpu.VMEM((2 * PEER_HEADS, ROUTE_TOK_BLOCK, PEER_DQ_HALF), jnp.bfloat16)],
        compiler_params=pltpu.CompilerParams(dimension_semantics=("arbitrary",),
                                             vmem_limit_bytes=DENSE_VMEM_LIMIT),
        name="peer_route",
    )(x, g_ffn.reshape(1, D_MODEL), _bf16(w_pq),
      _bf16(peer_keys.reshape(2 * PEER_HEADS, N_KEYS, PEER_DQ_HALF)))


EXPERT_ROWS = SUBLANES // 2
CHUNKS = D_MODEL // LANES
CHUNKS_LOG2 = CHUNKS.bit_length() - 1
PEER_K = PEER_SLOTS * CHUNKS
PEER_TOK_BLOCK = 256
STAGE_TOKENS = 8
PACK_ROW_BLOCK = 4096
PEER_TABLE_BYTES = N_EXPERTS * EXPERT_ROWS * LANES * 4
PEER_VMEM_LIMIT = PEER_TABLE_BYTES + (20 << 20)


def _peer_pack_kernel(x_ref, o_ref):
    f32 = jnp.float32
    half = PACK_ROW_BLOCK // 2
    lo = pltpu.bitcast(_bf16(x_ref[pl.ds(0, half, stride=2), :]).astype(f32), jnp.uint32)
    hi = pltpu.bitcast(_bf16(x_ref[pl.ds(1, half, stride=2), :]).astype(f32), jnp.uint32)
    o_ref[...] = (hi & jnp.uint32(0xFFFF0000)) | (lo >> 16)


def peer_pack_table(table):
    t2 = table.reshape(N_EXPERTS * CHUNKS, LANES)
    return pl.pallas_call(
        _peer_pack_kernel,
        grid=(N_EXPERTS * CHUNKS // PACK_ROW_BLOCK,),
        in_specs=[pl.BlockSpec((PACK_ROW_BLOCK, LANES), lambda i: (i, 0))],
        out_specs=pl.BlockSpec((PACK_ROW_BLOCK // 2, LANES), lambda i: (i, 0)),
        out_shape=jax.ShapeDtypeStruct((N_EXPERTS * EXPERT_ROWS, LANES), jnp.uint32),
        name="peer_pack",
    )(t2)


def _slot_expander():
    j = lax.broadcasted_iota(jnp.int32, (PEER_SLOTS, PEER_K), 0)
    c = lax.broadcasted_iota(jnp.int32, (PEER_SLOTS, PEER_K), 1)
    return jnp.where((c >> CHUNKS_LOG2) == j, 1.0, 0.0)


def _to_token_major(a):
    return jnp.concatenate([a[:, g * LANES:(g + 1) * LANES].T for g in range(a.shape[1] // LANES)], axis=0)


def _chunk_diagonal():
    s_out = lax.broadcasted_iota(jnp.int32, (SUBLANES, PEER_K), 0)
    c = lax.broadcasted_iota(jnp.int32, (SUBLANES, PEER_K), 1)
    return (c & (CHUNKS - 1)) == s_out


def _gather_tiles(row_of, tab_ref):
    tiles = []
    for j in range(0, PEER_SLOTS, 2):
        pair = [tab_ref[pl.ds(pl.multiple_of(row_of(j + k), EXPERT_ROWS), EXPERT_ROWS), :] for k in range(2)]
        tiles.append(pltpu.bitcast(jnp.concatenate(pair, axis=0), jnp.bfloat16))
    return jnp.concatenate(tiles, axis=0)


def _staged_token_loop(rows_ref, idx_smem, sem, process):
    n_groups = PEER_TOK_BLOCK // STAGE_TOKENS

    def fetch(group, buf):
        src = rows_ref.at[pl.ds(pl.multiple_of(group * STAGE_TOKENS, STAGE_TOKENS), STAGE_TOKENS)]
        return pltpu.make_async_copy(src, idx_smem.at[buf], sem.at[buf])

    for buf in range(2):
        fetch(buf, buf).start()

    def trip(i, carry):
        for buf in range(2):
            group = 2 * i + buf
            fetch(group, buf).wait()
            for u in range(STAGE_TOKENS):
                process(group * STAGE_TOKENS + u, lambda j, u=u: idx_smem[buf, u, j])
            fetch((group + 2) & (n_groups - 1), buf).start()
        return carry

    lax.fori_loop(0, n_groups // 2, trip, 0)
    for buf in range(2):
        fetch(buf, buf).wait()


def _split_bf16_rows(a):
    hi = _bf16(a).astype(jnp.float32)
    return _bf16(jnp.concatenate([hi, a - hi], axis=0))


def _peer_u_kernel(rows_ref, x_ref, tab_ref, act_ref, r_scr, idx_smem, sem):
    diag = _chunk_diagonal()

    def process(t, row_of):
        m = _gather_tiles(row_of, tab_ref)
        res = lax.dot_general(_split_bf16_rows(x_ref[t]), m, NT_DIMS, preferred_element_type=jnp.float32)
        r = jnp.where(diag, res[:SUBLANES] + res[SUBLANES:], 0.0)
        r_scr[pl.ds(t, 1), :] = jnp.sum(r, axis=0, keepdims=True)

    _staged_token_loop(rows_ref, idx_smem, sem, process)
    act_ref[...] = lax.dot_general(_slot_expander(), r_scr[...], NT_DIMS, precision=HIGHEST,
                                   preferred_element_type=jnp.float32)


def _peer_v_kernel(rows_ref, act_ref, gate_ref, tab_ref, out_ref, whi_scr, wlo_scr, idx_smem, sem):
    act = act_ref[...]
    w = _to_token_major(gate_ref[...] * (0.5 * act * (1.0 + lax.erf(act * (2.0 ** -0.5)))))
    expand = _bf16(_slot_expander())
    w_hi = _bf16(w)
    w_lo = _bf16(w - w_hi.astype(jnp.float32))
    whi_scr[...] = jnp.dot(w_hi, expand, preferred_element_type=jnp.float32)
    wlo_scr[...] = jnp.dot(w_lo, expand, preferred_element_type=jnp.float32)
    diag = _chunk_diagonal()

    def process(t, row_of):
        m = _gather_tiles(row_of, tab_ref)
        lhs = jnp.concatenate([jnp.where(diag, whi_scr[pl.ds(t, 1), :], 0.0),
                               jnp.where(diag, wlo_scr[pl.ds(t, 1), :], 0.0)], axis=0)
        res = jnp.dot(_bf16(lhs), m, preferred_element_type=jnp.float32)
        out_ref[t] = res[:SUBLANES] + res[SUBLANES:]

    _staged_token_loop(rows_ref, idx_smem, sem, process)


_PEER_PARAMS = pltpu.CompilerParams(dimension_semantics=("arbitrary",), vmem_limit_bytes=PEER_VMEM_LIMIT)


def peer_experts(xn, e_t, g_t, peer_u, peer_v):
    n = xn.shape[0]
    rows = (e_t * EXPERT_ROWS).T
    tok_rows = pl.BlockSpec((PEER_TOK_BLOCK, PEER_SLOTS), lambda i: (i, 0))
    slots = pl.BlockSpec((PEER_SLOTS, PEER_TOK_BLOCK), lambda i: (0, i))
    tiles = pl.BlockSpec((PEER_TOK_BLOCK, SUBLANES, LANES), lambda i: (i, 0, 0))
    table = _resident((N_EXPERTS * EXPERT_ROWS, LANES))
    wide = pltpu.VMEM((PEER_TOK_BLOCK, PEER_K), jnp.float32)
    staging = [pltpu.SMEM((2, STAGE_TOKENS, PEER_SLOTS), jnp.int32), pltpu.SemaphoreType.DMA((2,))]
    act = pl.pallas_call(
        _peer_u_kernel,
        grid=(n // PEER_TOK_BLOCK,),
        in_specs=[tok_rows, tiles, table],
        out_specs=slots,
        out_shape=jax.ShapeDtypeStruct((PEER_SLOTS, n), jnp.float32),
        scratch_shapes=[wide] + staging,
        compiler_params=_PEER_PARAMS,
        name="peer_u",
    )(rows, xn.reshape(n, SUBLANES, LANES), peer_pack_table(peer_u))
    out = pl.pallas_call(
        _peer_v_kernel,
        grid=(n // PEER_TOK_BLOCK,),
        in_specs=[tok_rows, slots, slots, table],
        out_specs=tiles,
        out_shape=jax.ShapeDtypeStruct((n, SUBLANES, LANES), jnp.float32),
        scratch_shapes=[wide, wide] + staging,
        compiler_params=_PEER_PARAMS,
        name="peer_v",
    )(rows, act, g_t, peer_pack_table(peer_v))
    return out.reshape(n, D_MODEL)


def _ple_kernel(x_ref, o_ref, p_ref, gple_ref, wpg_ref, wple_ref, gfin_ref, y_ref):
    x = x_ref[...] + o_ref[...]
    gate = jax.nn.sigmoid(_dot(_rms(x, gple_ref[...]), wpg_ref[...]))
    x = x + _dot(p_ref[...], wple_ref[...]) * gate
    y_ref[...] = _rms(x, gfin_ref[...])


def ple_and_final_norm(x, peer_out, p, g_ple, w_pg, w_ple, g_final):
    n = x.shape[0]
    row = lambda i: (i, 0)
    blk = pl.BlockSpec((ROW_BLOCK, D_MODEL), row)
    return pl.pallas_call(
        _ple_kernel,
        grid=(n // ROW_BLOCK,),
        in_specs=[blk, blk, pl.BlockSpec((ROW_BLOCK, D_PLE), row), _resident((1, D_MODEL)),
                  _resident((D_MODEL, D_MODEL)), _resident((D_PLE, D_MODEL)), _resident((1, D_MODEL))],
        out_specs=blk,
        out_shape=jax.ShapeDtypeStruct((n, D_MODEL), jnp.float32),
        compiler_params=pltpu.CompilerParams(dimension_semantics=("arbitrary",),
                                             vmem_limit_bytes=DENSE_VMEM_LIMIT),
        name="ple_final",
    )(x, peer_out, p, g_ple.reshape(1, D_MODEL), _bf16(w_pg), _bf16(w_ple), g_final.reshape(1, D_MODEL))


def kernel(x_prompt, x_sample, p_prompt, p_sample, state_ret, state_mlstm_C, state_mlstm_n,
           state_mlstm_m, state_conv, g_mix, w_in, g_ret_gn, w_mq, w_mk, conv_w, conv_b, b_i, b_f,
           g_ml_gn, w_skip, w_up_r, w_up_m, w_out, g_ffn, w_pq, peer_keys, peer_u, peer_v,
           g_ple, w_pg, w_ple, g_final):
    f32 = jnp.float32
    Bp = x_prompt.shape[0]
    pos_p = jnp.arange(x_prompt.shape[1], dtype=jnp.int32)
    pos_s = PAST_LEN + jnp.arange(x_sample.shape[1], dtype=jnp.int32)
    z_ret = jnp.zeros((Bp, H_RET, DH_RET, DH_RET), f32)
    z_c = jnp.zeros((Bp, H_ML, DH_ML, DH_ML), f32)
    z_n = jnp.zeros((Bp, H_ML, DH_ML), f32)
    z_m = jnp.zeros((Bp, H_ML), f32)
    z_buf = jnp.zeros((Bp, CONV_W - 1, W_ML), x_prompt.dtype)
    mixer_w = (g_ret_gn[0], w_mq[0], w_mk[0], conv_w[0], conv_b[0], b_i[0], b_f[0], g_ml_gn[0], w_skip[0])

    def stream(x, pos, s0, c0, n0, m0, cbuf):
        B, T, _ = x.shape
        x2 = x.reshape(B * T, D_MODEL)
        zr, zm, zif, zg = in_proj(x2, g_mix[0], w_in[0])
        seq = lambda z: z.reshape(B, T, z.shape[-1])
        y_r, y_m, *states = token_mixers(seq(zr), seq(zm), seq(zif), pos, s0, c0, n0, m0, cbuf, *mixer_w)
        x1 = merge_branches(x2, y_r.reshape(B * T, W_RET), y_m.reshape(B * T, W_ML), zg,
                            w_up_r[0], w_up_m[0], w_out[0])
        return x1, states

    hp, st_p = stream(x_prompt, pos_p, z_ret, z_c, z_n, z_m, z_buf)
    hs, st_s = stream(x_sample, pos_s, state_ret[0], state_mlstm_C[0], state_mlstm_n[0],
                      state_mlstm_m[0], state_conv[0])
    n_p = hp.shape[0]
    x_all = jnp.concatenate([hp, hs], axis=0)
    p_all = jnp.concatenate([p_prompt[0].reshape(n_p, D_PLE), p_sample[0].reshape(-1, D_PLE)], axis=0)
    xn, e_t, g_t = peer_route(x_all, g_ffn[0], w_pq[0], peer_keys[0])
    peer_out = peer_experts(xn, e_t, g_t, peer_u[0], peer_v[0])
    y_all = ple_and_final_norm(x_all, peer_out, p_all, g_ple[0], w_pg[0], w_ple[0], g_final)
    y_prompt = y_all[:n_p].reshape(x_prompt.shape)
    y_sample = y_all[n_p:].reshape(x_sample.shape)
    return (y_prompt, y_sample, *[a[None] for a in st_p], *[a[None] for a in st_s])
```

```python
import jax, jax.numpy as jnp
from jax import lax
import numpy as np
from jax.experimental import pallas as pl
from jax.experimental.pallas import tpu as pltpu

D_MODEL = 1024
PAST_LEN = 2048
CHUNK = 64
EPS = 1e-6
ROPE_BASE = 10000.0
H_RET = 4
W_RET = D_MODEL // 2
DH_RET = W_RET // H_RET
H_ML = 4
W_ML = D_MODEL // 2
DH_ML = W_ML // H_ML
CONV_W = 4
PEER_HEADS = 8
N_KEYS = 128
N_EXPERTS = N_KEYS * N_KEYS
PEER_TOPK = 16
PEER_DQ = 256
PEER_DQ_HALF = PEER_DQ // 2
D_PLE = 256

SUBLANES = 8
LANES = 128
NEG_INF = float("-inf")
HIGHEST = lax.Precision.HIGHEST
NT_DIMS = (((1,), (1,)), ((), ()))


def _bf16(a):
    return a.astype(jnp.bfloat16)


def _dot(a, b):
    return jnp.dot(_bf16(a), _bf16(b), preferred_element_type=jnp.float32)


def _dot_nt(a, b):
    return lax.dot_general(_bf16(a), _bf16(b), NT_DIMS, preferred_element_type=jnp.float32)


def _pad_rows(a, rows):
    return jnp.concatenate([a, jnp.zeros((rows - a.shape[0], a.shape[1]), a.dtype)], axis=0)


def _dot_tn(a, b):
    return _dot(_pad_rows(a, LANES).T, _pad_rows(b, LANES))


def _rms(x, g):
    return x * lax.rsqrt(jnp.mean(x * x, axis=-1, keepdims=True) + EPS) * g


def _resident(shape):
    return pl.BlockSpec(shape, lambda *_: (0,) * len(shape), pipeline_mode=pl.Buffered(1))


ROW_BLOCK = 256
DENSE_VMEM_LIMIT = 48 << 20
Z_RET = 4 * W_RET
Z_ML = 3 * W_ML
Z_GATE = 2 * D_MODEL
SCAN_BATCH = 4


def _in_proj_kernel(x_ref, g_ref, wr_ref, wm_ref, wif_ref, wg_ref, zr_ref, zm_ref, zif_ref, zg_ref):
    h = _bf16(_rms(x_ref[...], g_ref[...]))
    for w_ref, z_ref in ((wr_ref, zr_ref), (wm_ref, zm_ref), (wif_ref, zif_ref), (wg_ref, zg_ref)):
        z_ref[...] = jnp.dot(h, w_ref[...], preferred_element_type=jnp.float32)


def in_proj(x, g_mix, w_in):
    n = x.shape[0]
    w = _bf16(w_in)
    c_if = Z_RET + Z_ML
    w_if = jnp.pad(w[:, c_if:c_if + 2 * H_ML], ((0, 0), (0, LANES - 2 * H_ML)))
    widths = (Z_RET, Z_ML, LANES, Z_GATE)
    row = lambda i: (i, 0)
    return pl.pallas_call(
        _in_proj_kernel,
        grid=(n // ROW_BLOCK,),
        in_specs=[pl.BlockSpec((ROW_BLOCK, D_MODEL), row), _resident((1, D_MODEL))]
        + [_resident((D_MODEL, wd)) for wd in widths],
        out_specs=[pl.BlockSpec((ROW_BLOCK, wd), row) for wd in widths],
        out_shape=[jax.ShapeDtypeStruct((n, wd), jnp.float32) for wd in widths],
        compiler_params=pltpu.CompilerParams(dimension_semantics=("arbitrary",),
                                             vmem_limit_bytes=DENSE_VMEM_LIMIT),
        name="in_proj",
    )(x, g_mix.reshape(1, D_MODEL), w[:, :Z_RET], w[:, Z_RET:c_if], w_if, w[:, c_if + 2 * H_ML:])


def _head_norm(y):
    return y * lax.rsqrt(jnp.mean(y * y, axis=-1, keepdims=True) + EPS)


def _scan_kernel(zr_ref, zm_ref, zif_ref, cos_ref, sin_ref, s0_ref, c0_ref, n0_ref, m0_ref, cb0_ref,
                 gret_ref, wmq_ref, wmk_ref, convw_ref, convb_ref, bif_ref, gml_ref, wskip_ref,
                 yr_ref, ym_ref, sret_ref, cml_ref, nml_ref, mml_ref, ctail_ref,
                 s_scr, c_scr, n_scr, m_scr, xprev_scr):
    nb, L = zr_ref.shape[0], zr_ref.shape[1]
    chunk = pl.program_id(1)

    @pl.when(chunk == 0)
    def _():
        s_scr[...] = s0_ref[...]
        c_scr[...] = c0_ref[...]
        n_scr[...] = n0_ref[...]
        m_scr[...] = m0_ref[...]
        xprev_scr[...] = cb0_ref[...]

    f32 = jnp.float32
    row = lax.broadcasted_iota(jnp.int32, (L, L), 0)
    col = lax.broadcasted_iota(jnp.int32, (L, L), 1)
    causal = row >= col
    diff = (row - col).astype(f32)
    tril = jnp.where(causal, 1.0, 0.0).astype(f32)
    idx = lax.broadcasted_iota(jnp.int32, (L, 1), 0).astype(f32)
    cos, sin = cos_ref[...], sin_ref[...]
    half = DH_RET // 2

    for b in range(nb):
        zr = zr_ref[b]
        parts = []
        for h in range(H_RET):
            lg = float(np.log1p(-(2.0 ** (-5 - h))))
            sl = slice(h * DH_RET, (h + 1) * DH_RET)
            q = zr[:, sl]
            k = zr[:, W_RET + h * DH_RET:W_RET + (h + 1) * DH_RET]
            v = zr[:, 2 * W_RET + h * DH_RET:2 * W_RET + (h + 1) * DH_RET]
            gt = zr[:, 3 * W_RET + h * DH_RET:3 * W_RET + (h + 1) * DH_RET]
            q = q * cos + pltpu.roll(q, half, 1) * sin
            k = (k * cos + pltpu.roll(k, half, 1) * sin) * (DH_RET ** -0.5)
            dmask = jnp.where(causal, jnp.exp(lg * jnp.maximum(diff, 0.0)), 0.0)
            s_prev = s_scr[b, h]
            o = _dot(_dot_nt(q, k) * dmask, v)
            o = o + _dot(q * jnp.exp(lg * (idx + 1.0)), s_prev)
            ds = _dot_tn(k * jnp.exp(lg * (L - 1 - idx)), v)
            s_scr[b, h] = float(np.exp(lg * L)) * s_prev + ds
            parts.append(jax.nn.silu(gt) * (_head_norm(o) * gret_ref[:, sl]))
        yr_ref[b] = _bf16(jnp.concatenate(parts, axis=1))

        zm = zm_ref[b]
        xm = zm[:, :W_ML]
        xp = jnp.concatenate([xprev_scr[b], xm], axis=0)
        first = SUBLANES - (CONV_W - 1)
        xc = convb_ref[...] + xp[first:first + L] * convw_ref[0:1, :]
        for j in range(1, CONV_W):
            xc = xc + xp[first + j:first + j + L] * convw_ref[j:j + 1, :]
        xprev_scr[b] = xm[L - SUBLANES:, :]
        cact = jax.nn.silu(xc)
        gates = zif_ref[b] + bif_ref[...]
        fcum = jnp.dot(tril, jax.nn.log_sigmoid(gates), precision=HIGHEST, preferred_element_type=f32)
        gates_t = _pad_rows(gates, LANES).T
        fcum_t = _pad_rows(fcum, LANES).T
        parts = []
        for h in range(H_ML):
            sl = slice(h * DH_ML, (h + 1) * DH_ML)
            ch = cact[:, sl]
            q = _dot(ch, wmq_ref[h])
            k = _dot(ch, wmk_ref[h]) * (DH_ML ** -0.5)
            v = zm[:, W_ML + h * DH_ML:W_ML + (h + 1) * DH_ML]
            o_gate = zm[:, 2 * W_ML + h * DH_ML:2 * W_ML + (h + 1) * DH_ML]
            ig_row, f_row = gates_t[h:h + 1, :L], fcum_t[H_ML + h:H_ML + h + 1, :L]
            ig_col, f_col = gates[:, h:h + 1], fcum[:, H_ML + h:H_ML + h + 1]
            m_prev = m_scr[b, h:h + 1, 0:1]
            c_prev = c_scr[b, h]
            n_prev = n_scr[b, h:h + 1, :]
            log_d = jnp.where(causal, ig_row + f_col - f_row, NEG_INF)
            inter = m_prev + f_col
            m_t = jnp.maximum(inter, jnp.max(log_d, axis=1, keepdims=True))
            a = jnp.exp(inter - m_t)
            s = _dot_nt(q, k) * jnp.exp(log_d - m_t)
            num = _dot(s, v) + a * _dot(q, c_prev)
            den = jnp.sum(s, axis=1, keepdims=True) + a * jnp.sum(q * n_prev, axis=1, keepdims=True)
            hc = num / jnp.maximum(jnp.abs(den), jnp.exp(-m_t))
            m_new = m_t[L - 1:L, :]
            f_last = f_col[L - 1:L, :]
            a_end = jnp.exp(m_prev + f_last - m_new)
            kw = k * jnp.exp(ig_col + f_last - f_col - m_new)
            c_scr[b, h] = a_end * c_prev + _dot_tn(kw, v)
            n_scr[b, h:h + 1, :] = a_end * n_prev + jnp.sum(kw, axis=0, keepdims=True)
            m_scr[b, h:h + 1, :] = jnp.broadcast_to(m_new, (1, LANES))
            parts.append(jax.nn.sigmoid(o_gate) * (_head_norm(hc) * gml_ref[:, sl] + wskip_ref[:, sl] * ch))
        ym_ref[b] = _bf16(jnp.concatenate(parts, axis=1))

    @pl.when(chunk == pl.num_programs(1) - 1)
    def _():
        sret_ref[...] = s_scr[...]
        cml_ref[...] = c_scr[...]
        nml_ref[...] = n_scr[...]
        mml_ref[...] = m_scr[...]
        ctail_ref[...] = xprev_scr[...]


def _rope_tables(pos):
    inv = ROPE_BASE ** (-jnp.arange(DH_RET // 2, dtype=jnp.float32) / (DH_RET // 2))
    ang = pos.astype(jnp.float32)[:, None] * inv[None, :]
    cos, sin = jnp.cos(ang), jnp.sin(ang)
    return jnp.concatenate([cos, cos], axis=1), jnp.concatenate([-sin, sin], axis=1)


def token_mixers(zr, zm, zif, pos, s0, c0, n0, m0, conv_buf,
                 g_ret_gn, w_mq, w_mk, conv_w, conv_b, b_i, b_f, g_ml_gn, w_skip):
    B, T = zr.shape[0], zr.shape[1]
    L = min(CHUNK, T)
    nb = SCAN_BATCH
    cos2, sin2 = _rope_tables(pos)
    f32 = jnp.float32
    m0b = jnp.broadcast_to(m0.astype(f32)[:, :, None], (B, H_ML, LANES))
    cb0 = jnp.pad(conv_buf.astype(f32), ((0, 0), (SUBLANES - (CONV_W - 1), 0), (0, 0)))
    bif = jnp.pad(jnp.concatenate([b_i, b_f]).astype(f32), (0, LANES - 2 * H_ML)).reshape(1, LANES)
    seq = lambda wd: pl.BlockSpec((nb, L, wd), lambda i, c: (i, c, 0))
    tab = pl.BlockSpec((L, DH_RET), lambda i, c: (c, 0))
    st4 = pl.BlockSpec((nb, H_RET, DH_RET, DH_RET), lambda i, c: (i, 0, 0, 0))
    st3 = pl.BlockSpec((nb, H_ML, DH_ML), lambda i, c: (i, 0, 0))
    cb = pl.BlockSpec((nb, SUBLANES, W_ML), lambda i, c: (i, 0, 0))
    row512 = _resident((1, W_ML))
    outs = pl.pallas_call(
        _scan_kernel,
        grid=(B // nb, T // L),
        in_specs=[seq(Z_RET), seq(Z_ML), seq(LANES), tab, tab, st4, st4, st3, st3, cb,
                  row512, _resident((H_ML, DH_ML, DH_ML)), _resident((H_ML, DH_ML, DH_ML)),
                  _resident((CONV_W, W_ML)), row512, _resident((1, LANES)), row512, row512],
        out_specs=[seq(W_RET), seq(W_ML), st4, st4, st3, st3, cb],
        out_shape=[jax.ShapeDtypeStruct((B, T, W_RET), jnp.bfloat16),
                   jax.ShapeDtypeStruct((B, T, W_ML), jnp.bfloat16),
                   jax.ShapeDtypeStruct((B, H_RET, DH_RET, DH_RET), f32),
                   jax.ShapeDtypeStruct((B, H_ML, DH_ML, DH_ML), f32),
                   jax.ShapeDtypeStruct((B, H_ML, DH_ML), f32),
                   jax.ShapeDtypeStruct((B, H_ML, LANES), f32),
                   jax.ShapeDtypeStruct((B, SUBLANES, W_ML), f32)],
        scratch_shapes=[pltpu.VMEM((nb, H_RET, DH_RET, DH_RET), f32),
                        pltpu.VMEM((nb, H_ML, DH_ML, DH_ML), f32),
                        pltpu.VMEM((nb, H_ML, DH_ML), f32),
                        pltpu.VMEM((nb, H_ML, LANES), f32),
                        pltpu.VMEM((nb, SUBLANES, W_ML), f32)],
        compiler_params=pltpu.CompilerParams(dimension_semantics=("arbitrary", "arbitrary"),
                                             vmem_limit_bytes=DENSE_VMEM_LIMIT),
        name="token_mixers",
    )(zr, zm, zif, cos2, sin2, s0.astype(f32), c0.astype(f32), n0.astype(f32), m0b, cb0,
      g_ret_gn.reshape(1, W_RET), _bf16(w_mq), _bf16(w_mk), conv_w, conv_b.reshape(1, W_ML), bif,
      g_ml_gn.reshape(1, W_ML), w_skip.reshape(1, W_ML))
    y_r, y_m, s_new, c_new, n_new, m_new, ctail = outs
    return y_r, y_m, s_new, c_new, n_new, m_new[:, :, 0], ctail[:, SUBLANES - (CONV_W - 1):, :]


def _merge_kernel(x_ref, yr_ref, ym_ref, zg_ref, wur_ref, wum_ref, wo_ref, o_ref):
    zg = zg_ref[...]
    up_r = jnp.dot(yr_ref[...], wur_ref[...], preferred_element_type=jnp.float32)
    up_m = jnp.dot(ym_ref[...], wum_ref[...], preferred_element_type=jnp.float32)
    merged = jax.nn.sigmoid(zg[:, :D_MODEL]) * up_r + jax.nn.sigmoid(zg[:, D_MODEL:]) * up_m
    o_ref[...] = x_ref[...] + jnp.dot(_bf16(merged), wo_ref[...], preferred_element_type=jnp.float32)


def merge_branches(x, y_r, y_m, zg, w_up_r, w_up_m, w_out):
    n = x.shape[0]
    row = lambda i: (i, 0)
    return pl.pallas_call(
        _merge_kernel,
        grid=(n // ROW_BLOCK,),
        in_specs=[pl.BlockSpec((ROW_BLOCK, D_MODEL), row), pl.BlockSpec((ROW_BLOCK, W_RET), row),
                  pl.BlockSpec((ROW_BLOCK, W_ML), row), pl.BlockSpec((ROW_BLOCK, Z_GATE), row),
                  _resident((W_RET, D_MODEL)), _resident((W_ML, D_MODEL)), _resident((D_MODEL, D_MODEL))],
        out_specs=pl.BlockSpec((ROW_BLOCK, D_MODEL), row),
        out_shape=jax.ShapeDtypeStruct((n, D_MODEL), jnp.float32),
        compiler_params=pltpu.CompilerParams(dimension_semantics=("arbitrary",),
                                             vmem_limit_bytes=DENSE_VMEM_LIMIT),
        name="merge_branches",
    )(x, y_r, y_m, zg, _bf16(w_up_r), _bf16(w_up_m), _bf16(w_out))


PEER_SLOTS = PEER_HEADS * PEER_TOPK
ROUTE_TOK_BLOCK = 256
_CAND_GROUPS = ((0, 0), (0, 8), (1, 0)) + tuple((a, 0) for a in range(2, 8))


def _take_top16_rows(s, code):
    big = jnp.float32(1e9)
    vals, codes = [], []
    for _ in range(PEER_TOPK):
        m = jnp.max(s, axis=0, keepdims=True)
        c = jnp.min(jnp.where(s == m, code, big), axis=0, keepdims=True)
        s = jnp.where(code == c, NEG_INF, s)
        vals.append(m)
        codes.append(c)
    return jnp.concatenate(vals, axis=0), jnp.concatenate(codes, axis=0)


def _lookup_rows(table, sel):
    out = jnp.zeros_like(sel)
    for a in range(PEER_TOPK):
        out = jnp.where(sel == jnp.float32(a), table[a:a + 1, :], out)
    return out


def _route_one_head(s1, s2):
    key_id = lax.broadcasted_iota(jnp.int32, (N_KEYS, LANES), 0).astype(jnp.float32)
    v1, i1 = _take_top16_rows(s1, key_id)
    v2, i2 = _take_top16_rows(s2, key_id)
    sub = lax.broadcasted_iota(jnp.int32, (SUBLANES, LANES), 0)
    subf = sub.astype(jnp.float32)
    cand, code = [], []
    for a, b0 in _CAND_GROUPS:
        c = v1[a:a + 1, :] + v2[b0:b0 + SUBLANES, :]
        n_valid = PEER_TOPK // (a + 1) - b0
        cand.append(c if n_valid >= SUBLANES else jnp.where(sub < n_valid, c, NEG_INF))
        code.append(subf + jnp.float32(a * PEER_TOPK + b0))
    cand.append(v1[SUBLANES:, :] + v2[0:1, :])
    code.append((subf + jnp.float32(SUBLANES)) * jnp.float32(PEER_TOPK))
    sc, cc = _take_top16_rows(jnp.concatenate(cand, axis=0), jnp.concatenate(code, axis=0))
    a_sel = jnp.floor(cc * jnp.float32(1.0 / PEER_TOPK))
    b_sel = cc - a_sel * jnp.float32(PEER_TOPK)
    e = _lookup_rows(i1, a_sel) * jnp.float32(N_KEYS) + _lookup_rows(i2, b_sel)
    ex = jnp.exp(sc - sc[0:1, :])
    gate = ex / jnp.sum(ex, axis=0, keepdims=True)
    return e.astype(jnp.int32), gate


def _peer_route_kernel(x_ref, g_ref, wpq_ref, keys_ref, xn_ref, e_ref, gate_ref, q_scr):
    xn = _rms(x_ref[...], g_ref[...])
    xn_ref[...] = xn
    q = jnp.dot(_bf16(xn), wpq_ref[...], preferred_element_type=jnp.float32)
    for c in range(2 * PEER_HEADS):
        q_scr[c] = _bf16(q[:, c * PEER_DQ_HALF:(c + 1) * PEER_DQ_HALF])

    def head_body(n, carry):
        rows = pl.ds(pl.multiple_of(n * PEER_TOPK, PEER_TOPK), PEER_TOPK)
        for lg in range(ROUTE_TOK_BLOCK // LANES):
            toks = slice(lg * LANES, (lg + 1) * LANES)
            s = [lax.dot_general(keys_ref[2 * n + h], q_scr[2 * n + h, toks, :], NT_DIMS,
                                 preferred_element_type=jnp.float32) for h in range(2)]
            e, gate = _route_one_head(s[0], s[1])
            e_ref[rows, toks] = e
            gate_ref[rows, toks] = gate
        return carry

    lax.fori_loop(0, PEER_HEADS, head_body, 0)


def peer_route(x, g_ffn, w_pq, peer_keys):
    n = x.shape[0]
    tok = lambda i: (i, 0)
    slots = pl.BlockSpec((PEER_SLOTS, ROUTE_TOK_BLOCK), lambda i: (0, i))
    return pl.pallas_call(
        _peer_route_kernel,
        grid=(n // ROUTE_TOK_BLOCK,),
        in_specs=[pl.BlockSpec((ROUTE_TOK_BLOCK, D_MODEL), tok),
                  _resident((1, D_MODEL)),
                  _resident((D_MODEL, PEER_HEADS * PEER_DQ)),
                  _resident((2 * PEER_HEADS, N_KEYS, PEER_DQ_HALF))],
        out_specs=[pl.BlockSpec((ROUTE_TOK_BLOCK, D_MODEL), tok), slots, slots],
        out_shape=[jax.ShapeDtypeStruct((n, D_MODEL), jnp.float32),
                   jax.ShapeDtypeStruct((PEER_SLOTS, n), jnp.int32),
                   jax.ShapeDtypeStruct((PEER_SLOTS, n), jnp.float32)],
        scratch_shapes=[pltpu.VMEM((2 * PEER_HEADS, ROUTE_TOK_BLOCK, PEER_DQ_HALF), jnp.bfloat16)],
        compiler_params=pltpu.CompilerParams(dimension_semantics=("arbitrary",),
                                             vmem_limit_bytes=DENSE_VMEM_LIMIT),
        name="peer_route",
    )(x, g_ffn.reshape(1, D_MODEL), _bf16(w_pq),
      _bf16(peer_keys.reshape(2 * PEER_HEADS, N_KEYS, PEER_DQ_HALF)))


EXPERT_ROWS = SUBLANES // 2
CHUNKS = D_MODEL // LANES
CHUNKS_LOG2 = CHUNKS.bit_length() - 1
PEER_K = PEER_SLOTS * CHUNKS
PEER_TOK_BLOCK = 256
STAGE_TOKENS = 32
PACK_ROW_BLOCK = 4096
PEER_TABLE_BYTES = N_EXPERTS * EXPERT_ROWS * LANES * 4
PEER_VMEM_LIMIT = PEER_TABLE_BYTES + (20 << 20)


def _peer_pack_kernel(x_ref, o_ref):
    f32 = jnp.float32
    half = PACK_ROW_BLOCK // 2
    lo = pltpu.bitcast(_bf16(x_ref[pl.ds(0, half, stride=2), :]).astype(f32), jnp.uint32)
    hi = pltpu.bitcast(_bf16(x_ref[pl.ds(1, half, stride=2), :]).astype(f32), jnp.uint32)
    o_ref[...] = (hi & jnp.uint32(0xFFFF0000)) | (lo >> 16)


def peer_pack_table(table):
    t2 = table.reshape(N_EXPERTS * CHUNKS, LANES)
    return pl.pallas_call(
        _peer_pack_kernel,
        grid=(N_EXPERTS * CHUNKS // PACK_ROW_BLOCK,),
        in_specs=[pl.BlockSpec((PACK_ROW_BLOCK, LANES), lambda i: (i, 0))],
        out_specs=pl.BlockSpec((PACK_ROW_BLOCK // 2, LANES), lambda i: (i, 0)),
        out_shape=jax.ShapeDtypeStruct((N_EXPERTS * EXPERT_ROWS, LANES), jnp.uint32),
        name="peer_pack",
    )(t2)


def _slot_expander():
    j = lax.broadcasted_iota(jnp.int32, (PEER_SLOTS, PEER_K), 0)
    c = lax.broadcasted_iota(jnp.int32, (PEER_SLOTS, PEER_K), 1)
    return jnp.where((c >> CHUNKS_LOG2) == j, 1.0, 0.0)


def _to_token_major(a):
    return jnp.concatenate([a[:, g * LANES:(g + 1) * LANES].T for g in range(a.shape[1] // LANES)], axis=0)


def _chunk_diagonal():
    s_out = lax.broadcasted_iota(jnp.int32, (SUBLANES, PEER_K), 0)
    c = lax.broadcasted_iota(jnp.int32, (SUBLANES, PEER_K), 1)
    return (c & (CHUNKS - 1)) == s_out


def _gather_tiles(row_of, tab_ref):
    tiles = []
    for j in range(0, PEER_SLOTS, 2):
        pair = [tab_ref[pl.ds(pl.multiple_of(row_of(j + k), EXPERT_ROWS), EXPERT_ROWS), :] for k in range(2)]
        tiles.append(pltpu.bitcast(jnp.concatenate(pair, axis=0), jnp.bfloat16))
    return jnp.concatenate(tiles, axis=0)


def _staged_token_loop(rows_ref, idx_smem, sem, process):
    n_groups = PEER_TOK_BLOCK // STAGE_TOKENS

    def fetch(group, buf):
        src = rows_ref.at[pl.ds(pl.multiple_of(group * STAGE_TOKENS, STAGE_TOKENS), STAGE_TOKENS)]
        return pltpu.make_async_copy(src, idx_smem.at[buf], sem.at[buf])

    for buf in range(2):
        fetch(buf, buf).start()

    def trip(i, carry):
        for buf in range(2):
            group = 2 * i + buf
            fetch(group, buf).wait()
            for u in range(STAGE_TOKENS):
                process(group * STAGE_TOKENS + u, lambda j, u=u: idx_smem[buf, u, j])
            fetch((group + 2) & (n_groups - 1), buf).start()
        return carry

    lax.fori_loop(0, n_groups // 2, trip, 0)
    for buf in range(2):
        fetch(buf, buf).wait()


def _split_bf16_rows(a):
    hi = _bf16(a).astype(jnp.float32)
    return _bf16(jnp.concatenate([hi, a - hi], axis=0))


def _peer_u_kernel(rows_ref, x_ref, tab_ref, act_ref, r_scr, idx_smem, sem):
    diag = _chunk_diagonal()

    def process(t, row_of):
        m = _gather_tiles(row_of, tab_ref)
        res = lax.dot_general(_split_bf16_rows(x_ref[t]), m, NT_DIMS, preferred_element_type=jnp.float32)
        r = jnp.where(diag, res[:SUBLANES] + res[SUBLANES:], 0.0)
        r_scr[pl.ds(t, 1), :] = jnp.sum(r, axis=0, keepdims=True)

    _staged_token_loop(rows_ref, idx_smem, sem, process)
    act_ref[...] = lax.dot_general(_slot_expander(), r_scr[...], NT_DIMS, precision=HIGHEST,
                                   preferred_element_type=jnp.float32)


def _peer_v_kernel(rows_ref, act_ref, gate_ref, tab_ref, out_ref, whi_scr, wlo_scr, idx_smem, sem):
    act = act_ref[...]
    w = _to_token_major(gate_ref[...] * (0.5 * act * (1.0 + lax.erf(act * (2.0 ** -0.5)))))
    expand = _bf16(_slot_expander())
    w_hi = _bf16(w)
    w_lo = _bf16(w - w_hi.astype(jnp.float32))
    whi_scr[...] = jnp.dot(w_hi, expand, preferred_element_type=jnp.float32)
    wlo_scr[...] = jnp.dot(w_lo, expand, preferred_element_type=jnp.float32)
    diag = _chunk_diagonal()

    def process(t, row_of):
        m = _gather_tiles(row_of, tab_ref)
        lhs = jnp.concatenate([jnp.where(diag, whi_scr[pl.ds(t, 1), :], 0.0),
                               jnp.where(diag, wlo_scr[pl.ds(t, 1), :], 0.0)], axis=0)
        res = jnp.dot(_bf16(lhs), m, preferred_element_type=jnp.float32)
        out_ref[t] = res[:SUBLANES] + res[SUBLANES:]

    _staged_token_loop(rows_ref, idx_smem, sem, process)


_PEER_PARAMS = pltpu.CompilerParams(dimension_semantics=("arbitrary",), vmem_limit_bytes=PEER_VMEM_LIMIT)


def peer_experts(xn, e_t, g_t, peer_u, peer_v):
    n = xn.shape[0]
    rows = (e_t * EXPERT_ROWS).T
    tok_rows = pl.BlockSpec((PEER_TOK_BLOCK, PEER_SLOTS), lambda i: (i, 0))
    slots = pl.BlockSpec((PEER_SLOTS, PEER_TOK_BLOCK), lambda i: (0, i))
    tiles = pl.BlockSpec((PEER_TOK_BLOCK, SUBLANES, LANES), lambda i: (i, 0, 0))
    table = _resident((N_EXPERTS * EXPERT_ROWS, LANES))
    wide = pltpu.VMEM((PEER_TOK_BLOCK, PEER_K), jnp.float32)
    staging = [pltpu.SMEM((2, STAGE_TOKENS, PEER_SLOTS), jnp.int32), pltpu.SemaphoreType.DMA((2,))]
    act = pl.pallas_call(
        _peer_u_kernel,
        grid=(n // PEER_TOK_BLOCK,),
        in_specs=[tok_rows, tiles, table],
        out_specs=slots,
        out_shape=jax.ShapeDtypeStruct((PEER_SLOTS, n), jnp.float32),
        scratch_shapes=[wide] + staging,
        compiler_params=_PEER_PARAMS,
        name="peer_u",
    )(rows, xn.reshape(n, SUBLANES, LANES), peer_pack_table(peer_u))
    out = pl.pallas_call(
        _peer_v_kernel,
        grid=(n // PEER_TOK_BLOCK,),
        in_specs=[tok_rows, slots, slots, table],
        out_specs=tiles,
        out_shape=jax.ShapeDtypeStruct((n, SUBLANES, LANES), jnp.float32),
        scratch_shapes=[wide, wide] + staging,
        compiler_params=_PEER_PARAMS,
        name="peer_v",
    )(rows, act, g_t, peer_pack_table(peer_v))
    return out.reshape(n, D_MODEL)


def _ple_kernel(x_ref, o_ref, p_ref, gple_ref, wpg_ref, wple_ref, gfin_ref, y_ref):
    x = x_ref[...] + o_ref[...]
    gate = jax.nn.sigmoid(_dot(_rms(x, gple_ref[...]), wpg_ref[...]))
    x = x + _dot(p_ref[...], wple_ref[...]) * gate
    y_ref[...] = _rms(x, gfin_ref[...])


def ple_and_final_norm(x, peer_out, p, g_ple, w_pg, w_ple, g_final):
    n = x.shape[0]
    row = lambda i: (i, 0)
    blk = pl.BlockSpec((ROW_BLOCK, D_MODEL), row)
    return pl.pallas_call(
        _ple_kernel,
        grid=(n // ROW_BLOCK,),
        in_specs=[blk, blk, pl.BlockSpec((ROW_BLOCK, D_PLE), row), _resident((1, D_MODEL)),
                  _resident((D_MODEL, D_MODEL)), _resident((D_PLE, D_MODEL)), _resident((1, D_MODEL))],
        out_specs=blk,
        out_shape=jax.ShapeDtypeStruct((n, D_MODEL), jnp.float32),
        compiler_params=pltpu.CompilerParams(dimension_semantics=("arbitrary",),
                                             vmem_limit_bytes=DENSE_VMEM_LIMIT),
        name="ple_final",
    )(x, peer_out, p, g_ple.reshape(1, D_MODEL), _bf16(w_pg), _bf16(w_ple), g_final.reshape(1, D_MODEL))


def kernel(x_prompt, x_sample, p_prompt, p_sample, state_ret, state_mlstm_C, state_mlstm_n,
           state_mlstm_m, state_conv, g_mix, w_in, g_ret_gn, w_mq, w_mk, conv_w, conv_b, b_i, b_f,
           g_ml_gn, w_skip, w_up_r, w_up_m, w_out, g_ffn, w_pq, peer_keys, peer_u, peer_v,
           g_ple, w_pg, w_ple, g_final):
    f32 = jnp.float32
    Bp = x_prompt.shape[0]
    pos_p = jnp.arange(x_prompt.shape[1], dtype=jnp.int32)
    pos_s = PAST_LEN + jnp.arange(x_sample.shape[1], dtype=jnp.int32)
    z_ret = jnp.zeros((Bp, H_RET, DH_RET, DH_RET), f32)
    z_c = jnp.zeros((Bp, H_ML, DH_ML, DH_ML), f32)
    z_n = jnp.zeros((Bp, H_ML, DH_ML), f32)
    z_m = jnp.zeros((Bp, H_ML), f32)
    z_buf = jnp.zeros((Bp, CONV_W - 1, W_ML), x_prompt.dtype)
    mixer_w = (g_ret_gn[0], w_mq[0], w_mk[0], conv_w[0], conv_b[0], b_i[0], b_f[0], g_ml_gn[0], w_skip[0])

    def stream(x, pos, s0, c0, n0, m0, cbuf):
        B, T, _ = x.shape
        x2 = x.reshape(B * T, D_MODEL)
        zr, zm, zif, zg = in_proj(x2, g_mix[0], w_in[0])
        seq = lambda z: z.reshape(B, T, z.shape[-1])
        y_r, y_m, *states = token_mixers(seq(zr), seq(zm), seq(zif), pos, s0, c0, n0, m0, cbuf, *mixer_w)
        x1 = merge_branches(x2, y_r.reshape(B * T, W_RET), y_m.reshape(B * T, W_ML), zg,
                            w_up_r[0], w_up_m[0], w_out[0])
        return x1, states

    hp, st_p = stream(x_prompt, pos_p, z_ret, z_c, z_n, z_m, z_buf)
    hs, st_s = stream(x_sample, pos_s, state_ret[0], state_mlstm_C[0], state_mlstm_n[0],
                      state_mlstm_m[0], state_conv[0])
    n_p = hp.shape[0]
    x_all = jnp.concatenate([hp, hs], axis=0)
    p_all = jnp.concatenate([p_prompt[0].reshape(n_p, D_PLE), p_sample[0].reshape(-1, D_PLE)], axis=0)
    xn, e_t, g_t = peer_route(x_all, g_ffn[0], w_pq[0], peer_keys[0])
    peer_out = peer_experts(xn, e_t, g_t, peer_u[0], peer_v[0])
    y_all = ple_and_final_norm(x_all, peer_out, p_all, g_ple[0], w_pg[0], w_ple[0], g_final)
    y_prompt = y_all[:n_p].reshape(x_prompt.shape)
    y_sample = y_all[n_p:].reshape(x_sample.shape)
    return (y_prompt, y_sample, *[a[None] for a in st_p], *[a[None] for a in st_s])
```

```python
import jax, jax.numpy as jnp
from jax import lax
import numpy as np
from jax.experimental import pallas as pl
from jax.experimental.pallas import tpu as pltpu

D_MODEL = 1024
PAST_LEN = 2048
CHUNK = 64
EPS = 1e-6
ROPE_BASE = 10000.0
H_RET = 4
W_RET = D_MODEL // 2
DH_RET = W_RET // H_RET
H_ML = 4
W_ML = D_MODEL // 2
DH_ML = W_ML // H_ML
CONV_W = 4
PEER_HEADS = 8
N_KEYS = 128
N_EXPERTS = N_KEYS * N_KEYS
PEER_TOPK = 16
PEER_DQ = 256
PEER_DQ_HALF = PEER_DQ // 2
D_PLE = 256

SUBLANES = 8
LANES = 128
NEG_INF = float("-inf")
HIGHEST = lax.Precision.HIGHEST
NT_DIMS = (((1,), (1,)), ((), ()))


def _bf16(a):
    return a.astype(jnp.bfloat16)


def _dot(a, b):
    return jnp.dot(_bf16(a), _bf16(b), preferred_element_type=jnp.float32)


def _dot_nt(a, b):
    return lax.dot_general(_bf16(a), _bf16(b), NT_DIMS, preferred_element_type=jnp.float32)


def _pad_rows(a, rows):
    return jnp.concatenate([a, jnp.zeros((rows - a.shape[0], a.shape[1]), a.dtype)], axis=0)


def _dot_tn(a, b):
    return _dot(_pad_rows(a, LANES).T, _pad_rows(b, LANES))


def _rms(x, g):
    return x * lax.rsqrt(jnp.mean(x * x, axis=-1, keepdims=True) + EPS) * g


def _resident(shape):
    return pl.BlockSpec(shape, lambda *_: (0,) * len(shape), pipeline_mode=pl.Buffered(1))


ROW_BLOCK = 256
DENSE_VMEM_LIMIT = 48 << 20
Z_RET = 4 * W_RET
Z_ML = 3 * W_ML
Z_GATE = 2 * D_MODEL
SCAN_BATCH = 4


def _in_proj_kernel(x_ref, g_ref, wr_ref, wm_ref, wif_ref, wg_ref, zr_ref, zm_ref, zif_ref, zg_ref):
    h = _bf16(_rms(x_ref[...], g_ref[...]))
    for w_ref, z_ref in ((wr_ref, zr_ref), (wm_ref, zm_ref), (wif_ref, zif_ref), (wg_ref, zg_ref)):
        z_ref[...] = jnp.dot(h, w_ref[...], preferred_element_type=jnp.float32)


def in_proj(x, g_mix, w_in):
    n = x.shape[0]
    w = _bf16(w_in)
    c_if = Z_RET + Z_ML
    w_if = jnp.pad(w[:, c_if:c_if + 2 * H_ML], ((0, 0), (0, LANES - 2 * H_ML)))
    widths = (Z_RET, Z_ML, LANES, Z_GATE)
    row = lambda i: (i, 0)
    return pl.pallas_call(
        _in_proj_kernel,
        grid=(n // ROW_BLOCK,),
        in_specs=[pl.BlockSpec((ROW_BLOCK, D_MODEL), row), _resident((1, D_MODEL))]
        + [_resident((D_MODEL, wd)) for wd in widths],
        out_specs=[pl.BlockSpec((ROW_BLOCK, wd), row) for wd in widths],
        out_shape=[jax.ShapeDtypeStruct((n, wd), jnp.float32) for wd in widths],
        compiler_params=pltpu.CompilerParams(dimension_semantics=("arbitrary",),
                                             vmem_limit_bytes=DENSE_VMEM_LIMIT),
        name="in_proj",
    )(x, g_mix.reshape(1, D_MODEL), w[:, :Z_RET], w[:, Z_RET:c_if], w_if, w[:, c_if + 2 * H_ML:])


def _head_norm(y):
    return y * lax.rsqrt(jnp.mean(y * y, axis=-1, keepdims=True) + EPS)


def _scan_kernel(zr_ref, zm_ref, zif_ref, cos_ref, sin_ref, s0_ref, c0_ref, n0_ref, m0_ref, cb0_ref,
                 gret_ref, wmq_ref, wmk_ref, convw_ref, convb_ref, bif_ref, gml_ref, wskip_ref,
                 yr_ref, ym_ref, sret_ref, cml_ref, nml_ref, mml_ref, ctail_ref,
                 s_scr, c_scr, n_scr, m_scr, xprev_scr):
    nb, L = zr_ref.shape[0], zr_ref.shape[1]
    chunk = pl.program_id(1)

    @pl.when(chunk == 0)
    def _():
        s_scr[...] = s0_ref[...]
        c_scr[...] = c0_ref[...]
        n_scr[...] = n0_ref[...]
        m_scr[...] = m0_ref[...]
        xprev_scr[...] = cb0_ref[...]

    f32 = jnp.float32
    row = lax.broadcasted_iota(jnp.int32, (L, L), 0)
    col = lax.broadcasted_iota(jnp.int32, (L, L), 1)
    causal = row >= col
    diff = (row - col).astype(f32)
    tril = jnp.where(causal, 1.0, 0.0).astype(f32)
    idx = lax.broadcasted_iota(jnp.int32, (L, 1), 0).astype(f32)
    cos, sin = cos_ref[...], sin_ref[...]
    half = DH_RET // 2

    for b in range(nb):
        zr = zr_ref[b]
        parts = []
        for h in range(H_RET):
            lg = float(np.log1p(-(2.0 ** (-5 - h))))
            sl = slice(h * DH_RET, (h + 1) * DH_RET)
            q = zr[:, sl]
            k = zr[:, W_RET + h * DH_RET:W_RET + (h + 1) * DH_RET]
            v = zr[:, 2 * W_RET + h * DH_RET:2 * W_RET + (h + 1) * DH_RET]
            gt = zr[:, 3 * W_RET + h * DH_RET:3 * W_RET + (h + 1) * DH_RET]
            q = q * cos + pltpu.roll(q, half, 1) * sin
            k = (k * cos + pltpu.roll(k, half, 1) * sin) * (DH_RET ** -0.5)
            dmask = jnp.where(causal, jnp.exp(lg * jnp.maximum(diff, 0.0)), 0.0)
            s_prev = s_scr[b, h]
            o = _dot(_dot_nt(q, k) * dmask, v)
            o = o + _dot(q * jnp.exp(lg * (idx + 1.0)), s_prev)
            ds = _dot_tn(k * jnp.exp(lg * (L - 1 - idx)), v)
            s_scr[b, h] = float(np.exp(lg * L)) * s_prev + ds
            parts.append(jax.nn.silu(gt) * (_head_norm(o) * gret_ref[:, sl]))
        yr_ref[b] = _bf16(jnp.concatenate(parts, axis=1))

        zm = zm_ref[b]
        xm = zm[:, :W_ML]
        xp = jnp.concatenate([xprev_scr[b], xm], axis=0)
        first = SUBLANES - (CONV_W - 1)
        xc = convb_ref[...] + xp[first:first + L] * convw_ref[0:1, :]
        for j in range(1, CONV_W):
            xc = xc + xp[first + j:first + j + L] * convw_ref[j:j + 1, :]
        xprev_scr[b] = xm[L - SUBLANES:, :]
        cact = jax.nn.silu(xc)
        gates = zif_ref[b] + bif_ref[...]
        fcum = jnp.dot(tril, jax.nn.log_sigmoid(gates), precision=HIGHEST, preferred_element_type=f32)
        gates_t = _pad_rows(gates, LANES).T
        fcum_t = _pad_rows(fcum, LANES).T
        parts = []
        for h in range(H_ML):
            sl = slice(h * DH_ML, (h + 1) * DH_ML)
            ch = cact[:, sl]
            q = _dot(ch, wmq_ref[h])
            k = _dot(ch, wmk_ref[h]) * (DH_ML ** -0.5)
            v = zm[:, W_ML + h * DH_ML:W_ML + (h + 1) * DH_ML]
            o_gate = zm[:, 2 * W_ML + h * DH_ML:2 * W_ML + (h + 1) * DH_ML]
            ig_row, f_row = gates_t[h:h + 1, :L], fcum_t[H_ML + h:H_ML + h + 1, :L]
            ig_col, f_col = gates[:, h:h + 1], fcum[:, H_ML + h:H_ML + h + 1]
            m_prev = m_scr[b, h:h + 1, 0:1]
            c_prev = c_scr[b, h]
            n_prev = n_scr[b, h:h + 1, :]
            log_d = jnp.where(causal, ig_row + f_col - f_row, NEG_INF)
            inter = m_prev + f_col
            m_t = jnp.maximum(inter, jnp.max(log_d, axis=1, keepdims=True))
            a = jnp.exp(inter - m_t)
            s = _dot_nt(q, k) * jnp.exp(log_d - m_t)
            num = _dot(s, v) + a * _dot(q, c_prev)
            den = jnp.sum(s, axis=1, keepdims=True) + a * jnp.sum(q * n_prev, axis=1, keepdims=True)
            hc = num / jnp.maximum(jnp.abs(den), jnp.exp(-m_t))
            m_new = m_t[L - 1:L, :]
            f_last = f_col[L - 1:L, :]
            a_end = jnp.exp(m_prev + f_last - m_new)
            kw = k * jnp.exp(ig_col + f_last - f_col - m_new)
            c_scr[b, h] = a_end * c_prev + _dot_tn(kw, v)
            n_scr[b, h:h + 1, :] = a_end * n_prev + jnp.sum(kw, axis=0, keepdims=True)
            m_scr[b, h:h + 1, :] = jnp.broadcast_to(m_new, (1, LANES))
            parts.append(jax.nn.sigmoid(o_gate) * (_head_norm(hc) * gml_ref[:, sl] + wskip_ref[:, sl] * ch))
        ym_ref[b] = _bf16(jnp.concatenate(parts, axis=1))

    @pl.when(chunk == pl.num_programs(1) - 1)
    def _():
        sret_ref[...] = s_scr[...]
        cml_ref[...] = c_scr[...]
        nml_ref[...] = n_scr[...]
        mml_ref[...] = m_scr[...]
        ctail_ref[...] = xprev_scr[...]


def _rope_tables(pos):
    inv = ROPE_BASE ** (-jnp.arange(DH_RET // 2, dtype=jnp.float32) / (DH_RET // 2))
    ang = pos.astype(jnp.float32)[:, None] * inv[None, :]
    cos, sin = jnp.cos(ang), jnp.sin(ang)
    return jnp.concatenate([cos, cos], axis=1), jnp.concatenate([-sin, sin], axis=1)


def token_mixers(zr, zm, zif, pos, s0, c0, n0, m0, conv_buf,
                 g_ret_gn, w_mq, w_mk, conv_w, conv_b, b_i, b_f, g_ml_gn, w_skip):
    B, T = zr.shape[0], zr.shape[1]
    L = min(CHUNK, T)
    nb = SCAN_BATCH
    cos2, sin2 = _rope_tables(pos)
    f32 = jnp.float32
    m0b = jnp.broadcast_to(m0.astype(f32)[:, :, None], (B, H_ML, LANES))
    cb0 = jnp.pad(conv_buf.astype(f32), ((0, 0), (SUBLANES - (CONV_W - 1), 0), (0, 0)))
    bif = jnp.pad(jnp.concatenate([b_i, b_f]).astype(f32), (0, LANES - 2 * H_ML)).reshape(1, LANES)
    seq = lambda wd: pl.BlockSpec((nb, L, wd), lambda i, c: (i, c, 0))
    tab = pl.BlockSpec((L, DH_RET), lambda i, c: (c, 0))
    st4 = pl.BlockSpec((nb, H_RET, DH_RET, DH_RET), lambda i, c: (i, 0, 0, 0))
    st3 = pl.BlockSpec((nb, H_ML, DH_ML), lambda i, c: (i, 0, 0))
    cb = pl.BlockSpec((nb, SUBLANES, W_ML), lambda i, c: (i, 0, 0))
    row512 = _resident((1, W_ML))
    outs = pl.pallas_call(
        _scan_kernel,
        grid=(B // nb, T // L),
        in_specs=[seq(Z_RET), seq(Z_ML), seq(LANES), tab, tab, st4, st4, st3, st3, cb,
                  row512, _resident((H_ML, DH_ML, DH_ML)), _resident((H_ML, DH_ML, DH_ML)),
                  _resident((CONV_W, W_ML)), row512, _resident((1, LANES)), row512, row512],
        out_specs=[seq(W_RET), seq(W_ML), st4, st4, st3, st3, cb],
        out_shape=[jax.ShapeDtypeStruct((B, T, W_RET), jnp.bfloat16),
                   jax.ShapeDtypeStruct((B, T, W_ML), jnp.bfloat16),
                   jax.ShapeDtypeStruct((B, H_RET, DH_RET, DH_RET), f32),
                   jax.ShapeDtypeStruct((B, H_ML, DH_ML, DH_ML), f32),
                   jax.ShapeDtypeStruct((B, H_ML, DH_ML), f32),
                   jax.ShapeDtypeStruct((B, H_ML, LANES), f32),
                   jax.ShapeDtypeStruct((B, SUBLANES, W_ML), f32)],
        scratch_shapes=[pltpu.VMEM((nb, H_RET, DH_RET, DH_RET), f32),
                        pltpu.VMEM((nb, H_ML, DH_ML, DH_ML), f32),
                        pltpu.VMEM((nb, H_ML, DH_ML), f32),
                        pltpu.VMEM((nb, H_ML, LANES), f32),
                        pltpu.VMEM((nb, SUBLANES, W_ML), f32)],
        compiler_params=pltpu.CompilerParams(dimension_semantics=("arbitrary", "arbitrary"),
                                             vmem_limit_bytes=DENSE_VMEM_LIMIT),
        name="token_mixers",
    )(zr, zm, zif, cos2, sin2, s0.astype(f32), c0.astype(f32), n0.astype(f32), m0b, cb0,
      g_ret_gn.reshape(1, W_RET), _bf16(w_mq), _bf16(w_mk), conv_w, conv_b.reshape(1, W_ML), bif,
      g_ml_gn.reshape(1, W_ML), w_skip.reshape(1, W_ML))
    y_r, y_m, s_new, c_new, n_new, m_new, ctail = outs
    return y_r, y_m, s_new, c_new, n_new, m_new[:, :, 0], ctail[:, SUBLANES - (CONV_W - 1):, :]


def _merge_kernel(x_ref, yr_ref, ym_ref, zg_ref, wur_ref, wum_ref, wo_ref, o_ref):
    zg = zg_ref[...]
    up_r = jnp.dot(yr_ref[...], wur_ref[...], preferred_element_type=jnp.float32)
    up_m = jnp.dot(ym_ref[...], wum_ref[...], preferred_element_type=jnp.float32)
    merged = jax.nn.sigmoid(zg[:, :D_MODEL]) * up_r + jax.nn.sigmoid(zg[:, D_MODEL:]) * up_m
    o_ref[...] = x_ref[...] + jnp.dot(_bf16(merged), wo_ref[...], preferred_element_type=jnp.float32)


def merge_branches(x, y_r, y_m, zg, w_up_r, w_up_m, w_out):
    n = x.shape[0]
    row = lambda i: (i, 0)
    return pl.pallas_call(
        _merge_kernel,
        grid=(n // ROW_BLOCK,),
        in_specs=[pl.BlockSpec((ROW_BLOCK, D_MODEL), row), pl.BlockSpec((ROW_BLOCK, W_RET), row),
                  pl.BlockSpec((ROW_BLOCK, W_ML), row), pl.BlockSpec((ROW_BLOCK, Z_GATE), row),
                  _resident((W_RET, D_MODEL)), _resident((W_ML, D_MODEL)), _resident((D_MODEL, D_MODEL))],
        out_specs=pl.BlockSpec((ROW_BLOCK, D_MODEL), row),
        out_shape=jax.ShapeDtypeStruct((n, D_MODEL), jnp.float32),
        compiler_params=pltpu.CompilerParams(dimension_semantics=("arbitrary",),
                                             vmem_limit_bytes=DENSE_VMEM_LIMIT),
        name="merge_branches",
    )(x, y_r, y_m, zg, _bf16(w_up_r), _bf16(w_up_m), _bf16(w_out))


PEER_SLOTS = PEER_HEADS * PEER_TOPK
ROUTE_TOK_BLOCK = 256
_CAND_GROUPS = ((0, 0), (0, 8), (1, 0)) + tuple((a, 0) for a in range(2, 8))


def _take_top16_rows(s, code):
    big = jnp.float32(1e9)
    vals, codes = [], []
    for _ in range(PEER_TOPK):
        m = jnp.max(s, axis=0, keepdims=True)
        c = jnp.min(jnp.where(s == m, code, big), axis=0, keepdims=True)
        s = jnp.where(code == c, NEG_INF, s)
        vals.append(m)
        codes.append(c)
    return jnp.concatenate(vals, axis=0), jnp.concatenate(codes, axis=0)


def _oddeven_merge(lo, hi, r):
    step = r * 2
    if step < hi - lo:
        yield from _oddeven_merge(lo, hi, step)
        yield from _oddeven_merge(lo + r, hi, step)
        yield from [(i, i + r) for i in range(lo + r, hi - r, step)]
    else:
        yield (lo, lo + r)


def _oddeven_sort(lo, hi):
    if hi - lo >= 1:
        mid = lo + (hi - lo) // 2
        yield from _oddeven_sort(lo, mid)
        yield from _oddeven_sort(mid + 1, hi)
        yield from _oddeven_merge(lo, hi, 1)


_SORT16 = tuple(_oddeven_sort(0, PEER_TOPK - 1))


def _sort_columns(vals, codes):
    vals, codes = list(vals), list(codes)
    for i, j in _SORT16:
        va, vb, ca, cb = vals[i], vals[j], codes[i], codes[j]
        swap = (vb > va) | ((vb == va) & (cb < ca))
        vals[i], vals[j] = jnp.maximum(va, vb), jnp.minimum(va, vb)
        codes[i], codes[j] = jnp.where(swap, cb, ca), jnp.where(swap, ca, cb)
    return vals, codes


def _pop_top16(vals, codes, extra=None):
    vals, codes = list(vals), list(codes)
    big = jnp.float32(1e9)
    out_v, out_c = [], []
    for r in range(PEER_TOPK):
        hv, hc = vals[0], codes[0]
        m = jnp.max(hv, axis=0, keepdims=True)
        if extra is not None:
            m = jnp.maximum(m, jnp.max(extra[0], axis=0, keepdims=True))
        c = jnp.min(jnp.where(hv == m, hc, big), axis=0, keepdims=True)
        if extra is not None:
            c = jnp.minimum(c, jnp.min(jnp.where(extra[0] == m, extra[1], big), axis=0, keepdims=True))
            extra = (jnp.where(extra[1] == c, NEG_INF, extra[0]), extra[1])
        out_v.append(m)
        out_c.append(c)
        if r == PEER_TOPK - 1:
            break
        won = hc == c
        depth = min(len(vals) - 1, PEER_TOPK - 1 - r)
        for i in range(depth):
            vals[i] = jnp.where(won, vals[i + 1], vals[i])
            codes[i] = jnp.where(won, codes[i + 1], codes[i])
        vals[depth] = jnp.where(won, NEG_INF, vals[depth])
    return jnp.concatenate(out_v, axis=0), jnp.concatenate(out_c, axis=0)


def _top16_keys(s):
    sub = lax.broadcasted_iota(jnp.int32, (SUBLANES, LANES), 0).astype(jnp.float32)
    n = N_KEYS // SUBLANES
    vals = [s[i * SUBLANES:(i + 1) * SUBLANES, :] for i in range(n)]
    codes = [sub + jnp.float32(i * SUBLANES) for i in range(n)]
    return _pop_top16(*_sort_columns(vals, codes))


def _top16_pairs(v1, v2):
    sub = lax.broadcasted_iota(jnp.int32, (SUBLANES, LANES), 0)
    subf = sub.astype(jnp.float32)
    vals, codes = [], []
    for b in range(PEER_TOPK):
        n_valid = PEER_TOPK // (b + 1)
        c = v1[:SUBLANES, :] + v2[b:b + 1, :]
        vals.append(c if n_valid >= SUBLANES else jnp.where(sub < n_valid, c, NEG_INF))
        codes.append(subf * jnp.float32(PEER_TOPK) + jnp.float32(b))
    extra = (v1[SUBLANES:, :] + v2[0:1, :], (subf + jnp.float32(SUBLANES)) * jnp.float32(PEER_TOPK))
    return _pop_top16(vals, codes, extra)


def _lookup_rows(table, sel):
    out = jnp.zeros_like(sel)
    for a in range(PEER_TOPK):
        out = jnp.where(sel == jnp.float32(a), table[a:a + 1, :], out)
    return out


def _route_one_head(s1, s2):
    v1, i1 = _top16_keys(s1)
    v2, i2 = _top16_keys(s2)
    sc, cc = _top16_pairs(v1, v2)
    a_sel = jnp.floor(cc * jnp.float32(1.0 / PEER_TOPK))
    b_sel = cc - a_sel * jnp.float32(PEER_TOPK)
    e = _lookup_rows(i1, a_sel) * jnp.float32(N_KEYS) + _lookup_rows(i2, b_sel)
    ex = jnp.exp(sc - sc[0:1, :])
    gate = ex / jnp.sum(ex, axis=0, keepdims=True)
    return e.astype(jnp.int32), gate


def _peer_route_kernel(x_ref, g_ref, wpq_ref, keys_ref, xn_ref, e_ref, gate_ref, q_scr):
    xn = _rms(x_ref[...], g_ref[...])
    xn_ref[...] = xn
    q = jnp.dot(_bf16(xn), wpq_ref[...], preferred_element_type=jnp.float32)
    for c in range(2 * PEER_HEADS):
        q_scr[c] = _bf16(q[:, c * PEER_DQ_HALF:(c + 1) * PEER_DQ_HALF])

    def head_body(n, carry):
        rows = pl.ds(pl.multiple_of(n * PEER_TOPK, PEER_TOPK), PEER_TOPK)
        for lg in range(ROUTE_TOK_BLOCK // LANES):
            toks = slice(lg * LANES, (lg + 1) * LANES)
            s = [lax.dot_general(keys_ref[2 * n + h], q_scr[2 * n + h, toks, :], NT_DIMS,
                                 preferred_element_type=jnp.float32) for h in range(2)]
            e, gate = _route_one_head(s[0], s[1])
            e_ref[rows, toks] = e
            gate_ref[rows, toks] = gate
        return carry

    lax.fori_loop(0, PEER_HEADS, head_body, 0)


def peer_route(x, g_ffn, w_pq, peer_keys):
    n = x.shape[0]
    tok = lambda i: (i, 0)
    slots = pl.BlockSpec((PEER_SLOTS, ROUTE_TOK_BLOCK), lambda i: (0, i))
    return pl.pallas_call(
        _peer_route_kernel,
        grid=(n // ROUTE_TOK_BLOCK,),
        in_specs=[pl.BlockSpec((ROUTE_TOK_BLOCK, D_MODEL), tok),
                  _resident((1, D_MODEL)),
                  _resident((D_MODEL, PEER_HEADS * PEER_DQ)),
                  _resident((2 * PEER_HEADS, N_KEYS, PEER_DQ_HALF))],
        out_specs=[pl.BlockSpec((ROUTE_TOK_BLOCK, D_MODEL), tok), slots, slots],
        out_shape=[jax.ShapeDtypeStruct((n, D_MODEL), jnp.float32),
                   jax.ShapeDtypeStruct((PEER_SLOTS, n), jnp.int32),
                   jax.ShapeDtypeStruct((PEER_SLOTS, n), jnp.float32)],
        scratch_shapes=[pltpu.VMEM((2 * PEER_HEADS, ROUTE_TOK_BLOCK, PEER_DQ_HALF), jnp.bfloat16)],
        compiler_params=pltpu.CompilerParams(dimension_semantics=("arbitrary",),
                                             vmem_limit_bytes=DENSE_VMEM_LIMIT),
        name="peer_route",
    )(x, g_ffn.reshape(1, D_MODEL), _bf16(w_pq),
      _bf16(peer_keys.reshape(2 * PEER_HEADS, N_KEYS, PEER_DQ_HALF)))


EXPERT_ROWS = SUBLANES // 2
CHUNKS = D_MODEL // LANES
CHUNKS_LOG2 = CHUNKS.bit_length() - 1
PEER_K = PEER_SLOTS * CHUNKS
PEER_TOK_BLOCK = 256
STAGE_TOKENS = 32
PACK_ROW_BLOCK = 4096
PEER_TABLE_BYTES = N_EXPERTS * EXPERT_ROWS * LANES * 4
PEER_VMEM_LIMIT = PEER_TABLE_BYTES + (20 << 20)


def _peer_pack_kernel(x_ref, o_ref):
    f32 = jnp.float32
    half = PACK_ROW_BLOCK // 2
    lo = pltpu.bitcast(_bf16(x_ref[pl.ds(0, half, stride=2), :]).astype(f32), jnp.uint32)
    hi = pltpu.bitcast(_bf16(x_ref[pl.ds(1, half, stride=2), :]).astype(f32), jnp.uint32)
    o_ref[...] = (hi & jnp.uint32(0xFFFF0000)) | (lo >> 16)


def peer_pack_table(table):
    t2 = table.reshape(N_EXPERTS * CHUNKS, LANES)
    return pl.pallas_call(
        _peer_pack_kernel,
        grid=(N_EXPERTS * CHUNKS // PACK_ROW_BLOCK,),
        in_specs=[pl.BlockSpec((PACK_ROW_BLOCK, LANES), lambda i: (i, 0))],
        out_specs=pl.BlockSpec((PACK_ROW_BLOCK // 2, LANES), lambda i: (i, 0)),
        out_shape=jax.ShapeDtypeStruct((N_EXPERTS * EXPERT_ROWS, LANES), jnp.uint32),
        name="peer_pack",
    )(t2)


def _slot_expander():
    j = lax.broadcasted_iota(jnp.int32, (PEER_SLOTS, PEER_K), 0)
    c = lax.broadcasted_iota(jnp.int32, (PEER_SLOTS, PEER_K), 1)
    return jnp.where((c >> CHUNKS_LOG2) == j, 1.0, 0.0)


def _to_token_major(a):
    return jnp.concatenate([a[:, g * LANES:(g + 1) * LANES].T for g in range(a.shape[1] // LANES)], axis=0)


def _chunk_diagonal():
    s_out = lax.broadcasted_iota(jnp.int32, (SUBLANES, PEER_K), 0)
    c = lax.broadcasted_iota(jnp.int32, (SUBLANES, PEER_K), 1)
    return (c & (CHUNKS - 1)) == s_out


def _gather_tiles(row_of, tab_ref):
    tiles = []
    for j in range(0, PEER_SLOTS, 2):
        pair = [tab_ref[pl.ds(pl.multiple_of(row_of(j + k), EXPERT_ROWS), EXPERT_ROWS), :] for k in range(2)]
        tiles.append(pltpu.bitcast(jnp.concatenate(pair, axis=0), jnp.bfloat16))
    return jnp.concatenate(tiles, axis=0)


def _staged_token_loop(rows_ref, idx_smem, sem, process):
    n_groups = PEER_TOK_BLOCK // STAGE_TOKENS

    def fetch(group, buf):
        src = rows_ref.at[pl.ds(pl.multiple_of(group * STAGE_TOKENS, STAGE_TOKENS), STAGE_TOKENS)]
        return pltpu.make_async_copy(src, idx_smem.at[buf], sem.at[buf])

    for buf in range(2):
        fetch(buf, buf).start()

    def trip(i, carry):
        for buf in range(2):
            group = 2 * i + buf
            fetch(group, buf).wait()
            for u in range(STAGE_TOKENS):
                process(group * STAGE_TOKENS + u, lambda j, u=u: idx_smem[buf, u, j])
            fetch((group + 2) & (n_groups - 1), buf).start()
        return carry

    lax.fori_loop(0, n_groups // 2, trip, 0)
    for buf in range(2):
        fetch(buf, buf).wait()


def _split_bf16_rows(a):
    hi = _bf16(a).astype(jnp.float32)
    return _bf16(jnp.concatenate([hi, a - hi], axis=0))


def _peer_u_kernel(rows_ref, x_ref, tab_ref, act_ref, r_scr, idx_smem, sem):
    diag = _chunk_diagonal()

    def process(t, row_of):
        m = _gather_tiles(row_of, tab_ref)
        res = lax.dot_general(_split_bf16_rows(x_ref[t]), m, NT_DIMS, preferred_element_type=jnp.float32)
        r = jnp.where(diag, res[:SUBLANES] + res[SUBLANES:], 0.0)
        r_scr[pl.ds(t, 1), :] = jnp.sum(r, axis=0, keepdims=True)

    _staged_token_loop(rows_ref, idx_smem, sem, process)
    act_ref[...] = lax.dot_general(_slot_expander(), r_scr[...], NT_DIMS, precision=HIGHEST,
                                   preferred_element_type=jnp.float32)


def _peer_v_kernel(rows_ref, act_ref, gate_ref, tab_ref, out_ref, whi_scr, wlo_scr, idx_smem, sem):
    act = act_ref[...]
    w = _to_token_major(gate_ref[...] * (0.5 * act * (1.0 + lax.erf(act * (2.0 ** -0.5)))))
    expand = _bf16(_slot_expander())
    w_hi = _bf16(w)
    w_lo = _bf16(w - w_hi.astype(jnp.float32))
    whi_scr[...] = jnp.dot(w_hi, expand, preferred_element_type=jnp.float32)
    wlo_scr[...] = jnp.dot(w_lo, expand, preferred_element_type=jnp.float32)
    diag = _chunk_diagonal()

    def process(t, row_of):
        m = _gather_tiles(row_of, tab_ref)
        lhs = jnp.concatenate([jnp.where(diag, whi_scr[pl.ds(t, 1), :], 0.0),
                               jnp.where(diag, wlo_scr[pl.ds(t, 1), :], 0.0)], axis=0)
        res = jnp.dot(_bf16(lhs), m, preferred_element_type=jnp.float32)
        out_ref[t] = res[:SUBLANES] + res[SUBLANES:]

    _staged_token_loop(rows_ref, idx_smem, sem, process)


_PEER_PARAMS = pltpu.CompilerParams(dimension_semantics=("arbitrary",), vmem_limit_bytes=PEER_VMEM_LIMIT)


def peer_experts(xn, e_t, g_t, peer_u, peer_v):
    n = xn.shape[0]
    rows = (e_t * EXPERT_ROWS).T
    tok_rows = pl.BlockSpec((PEER_TOK_BLOCK, PEER_SLOTS), lambda i: (i, 0))
    slots = pl.BlockSpec((PEER_SLOTS, PEER_TOK_BLOCK), lambda i: (0, i))
    tiles = pl.BlockSpec((PEER_TOK_BLOCK, SUBLANES, LANES), lambda i: (i, 0, 0))
    table = _resident((N_EXPERTS * EXPERT_ROWS, LANES))
    wide = pltpu.VMEM((PEER_TOK_BLOCK, PEER_K), jnp.float32)
    staging = [pltpu.SMEM((2, STAGE_TOKENS, PEER_SLOTS), jnp.int32), pltpu.SemaphoreType.DMA((2,))]
    act = pl.pallas_call(
        _peer_u_kernel,
        grid=(n // PEER_TOK_BLOCK,),
        in_specs=[tok_rows, tiles, table],
        out_specs=slots,
        out_shape=jax.ShapeDtypeStruct((PEER_SLOTS, n), jnp.float32),
        scratch_shapes=[wide] + staging,
        compiler_params=_PEER_PARAMS,
        name="peer_u",
    )(rows, xn.reshape(n, SUBLANES, LANES), peer_pack_table(peer_u))
    out = pl.pallas_call(
        _peer_v_kernel,
        grid=(n // PEER_TOK_BLOCK,),
        in_specs=[tok_rows, slots, slots, table],
        out_specs=tiles,
        out_shape=jax.ShapeDtypeStruct((n, SUBLANES, LANES), jnp.float32),
        scratch_shapes=[wide, wide] + staging,
        compiler_params=_PEER_PARAMS,
        name="peer_v",
    )(rows, act, g_t, peer_pack_table(peer_v))
    return out.reshape(n, D_MODEL)


def _ple_kernel(x_ref, o_ref, p_ref, gple_ref, wpg_ref, wple_ref, gfin_ref, y_ref):
    x = x_ref[...] + o_ref[...]
    gate = jax.nn.sigmoid(_dot(_rms(x, gple_ref[...]), wpg_ref[...]))
    x = x + _dot(p_ref[...], wple_ref[...]) * gate
    y_ref[...] = _rms(x, gfin_ref[...])


def ple_and_final_norm(x, peer_out, p, g_ple, w_pg, w_ple, g_final):
    n = x.shape[0]
    row = lambda i: (i, 0)
    blk = pl.BlockSpec((ROW_BLOCK, D_MODEL), row)
    return pl.pallas_call(
        _ple_kernel,
        grid=(n // ROW_BLOCK,),
        in_specs=[blk, blk, pl.BlockSpec((ROW_BLOCK, D_PLE), row), _resident((1, D_MODEL)),
                  _resident((D_MODEL, D_MODEL)), _resident((D_PLE, D_MODEL)), _resident((1, D_MODEL))],
        out_specs=blk,
        out_shape=jax.ShapeDtypeStruct((n, D_MODEL), jnp.float32),
        compiler_params=pltpu.CompilerParams(dimension_semantics=("arbitrary",),
                                             vmem_limit_bytes=DENSE_VMEM_LIMIT),
        name="ple_final",
    )(x, peer_out, p, g_ple.reshape(1, D_MODEL), _bf16(w_pg), _bf16(w_ple), g_final.reshape(1, D_MODEL))


def kernel(x_prompt, x_sample, p_prompt, p_sample, state_ret, state_mlstm_C, state_mlstm_n,
           state_mlstm_m, state_conv, g_mix, w_in, g_ret_gn, w_mq, w_mk, conv_w, conv_b, b_i, b_f,
           g_ml_gn, w_skip, w_up_r, w_up_m, w_out, g_ffn, w_pq, peer_keys, peer_u, peer_v,
           g_ple, w_pg, w_ple, g_final):
    f32 = jnp.float32
    Bp = x_prompt.shape[0]
    pos_p = jnp.arange(x_prompt.shape[1], dtype=jnp.int32)
    pos_s = PAST_LEN + jnp.arange(x_sample.shape[1], dtype=jnp.int32)
    z_ret = jnp.zeros((Bp, H_RET, DH_RET, DH_RET), f32)
    z_c = jnp.zeros((Bp, H_ML, DH_ML, DH_ML), f32)
    z_n = jnp.zeros((Bp, H_ML, DH_ML), f32)
    z_m = jnp.zeros((Bp, H_ML), f32)
    z_buf = jnp.zeros((Bp, CONV_W - 1, W_ML), x_prompt.dtype)
    mixer_w = (g_ret_gn[0], w_mq[0], w_mk[0], conv_w[0], conv_b[0], b_i[0], b_f[0], g_ml_gn[0], w_skip[0])

    def stream(x, pos, s0, c0, n0, m0, cbuf):
        B, T, _ = x.shape
        x2 = x.reshape(B * T, D_MODEL)
        zr, zm, zif, zg = in_proj(x2, g_mix[0], w_in[0])
        seq = lambda z: z.reshape(B, T, z.shape[-1])
        y_r, y_m, *states = token_mixers(seq(zr), seq(zm), seq(zif), pos, s0, c0, n0, m0, cbuf, *mixer_w)
        x1 = merge_branches(x2, y_r.reshape(B * T, W_RET), y_m.reshape(B * T, W_ML), zg,
                            w_up_r[0], w_up_m[0], w_out[0])
        return x1, states

    hp, st_p = stream(x_prompt, pos_p, z_ret, z_c, z_n, z_m, z_buf)
    hs, st_s = stream(x_sample, pos_s, state_ret[0], state_mlstm_C[0], state_mlstm_n[0],
                      state_mlstm_m[0], state_conv[0])
    n_p = hp.shape[0]
    x_all = jnp.concatenate([hp, hs], axis=0)
    p_all = jnp.concatenate([p_prompt[0].reshape(n_p, D_PLE), p_sample[0].reshape(-1, D_PLE)], axis=0)
    xn, e_t, g_t = peer_route(x_all, g_ffn[0], w_pq[0], peer_keys[0])
    peer_out = peer_experts(xn, e_t, g_t, peer_u[0], peer_v[0])
    y_all = ple_and_final_norm(x_all, peer_out, p_all, g_ple[0], w_pg[0], w_ple[0], g_final)
    y_prompt = y_all[:n_p].reshape(x_prompt.shape)
    y_sample = y_all[n_p:].reshape(x_sample.shape)
    return (y_prompt, y_sample, *[a[None] for a in st_p], *[a[None] for a in st_s])
```
